```python
import math
import jax
import jax.numpy as jnp
from jax import lax
import numpy as np

D_MODEL = 4096
BATCH = 4
SEQ = 2048
DEPTH = 2
DEC_BATCH = 8
DEC_SEQ = 1
PAST_LEN = 16384
PAGE_SIZE = 128

ATTN_HEAD_DIM = 128
ATTN_HEADS = D_MODEL // 256
ATTN_WIDTH = ATTN_HEADS * ATTN_HEAD_DIM
Q_BLOCK = 128
FORGET_BIAS_INIT = 4.0
CONV_WIDTH = D_MODEL // 2
CONV_K = 3
MLSTM_HEADS = 8
MLSTM_HEAD_DIM = D_MODEL // 16
MLSTM_WIDTH = MLSTM_HEADS * MLSTM_HEAD_DIM
MLSTM_CHUNK = 64
MLSTM_F_BIAS_INIT = 3.0
N_GROUPS = 4
EXPERTS_PER_GROUP = 8
N_EXPERTS = N_GROUPS * EXPERTS_PER_GROUP
TOP_K_INNER = 2
D_EXPERT = D_MODEL // 4
MOE_BLOCK = 512
DEEPNORM_ALPHA = (2 * DEPTH) ** 0.25
DEEPNORM_BETA = (8 * DEPTH) ** -0.25
LN_EPS = 1e-5

MIX_WIDTH = ATTN_WIDTH + CONV_WIDTH + MLSTM_WIDTH
IN_SPLITS = (ATTN_WIDTH, ATTN_WIDTH, ATTN_WIDTH, ATTN_HEADS,
             CONV_WIDTH, CONV_WIDTH, CONV_WIDTH,
             MLSTM_WIDTH, MLSTM_WIDTH, MLSTM_WIDTH, MLSTM_WIDTH, MLSTM_HEADS, MLSTM_HEADS,
             D_MODEL, D_MODEL, D_MODEL)
IN_COLS = sum(IN_SPLITS)

kernel_name = 'fox_conv_mlstm_hmoe_deepnorm_step'

F32 = jnp.float32


def _split_offsets(sizes):
    return [int(o) for o in np.cumsum(np.array(sizes))[:-1]]


def layer_norm(x, g, b):
    xf = x.astype(F32)
    mu = jnp.mean(xf, axis=-1, keepdims=True)
    var = jnp.mean(jnp.square(xf - mu), axis=-1, keepdims=True)
    return ((xf - mu) * lax.rsqrt(var + LN_EPS) * g.astype(F32) + b.astype(F32)).astype(x.dtype)


def head_norm(h, g):
    mu = jnp.mean(h, axis=-1, keepdims=True)
    var = jnp.mean(jnp.square(h - mu), axis=-1, keepdims=True)
    return (h - mu) * lax.rsqrt(var + LN_EPS) * g.astype(F32).reshape(MLSTM_HEADS, MLSTM_HEAD_DIM)


def gather_pages(pool, page_table):
    g = pool[page_table]
    return g.reshape((g.shape[0], g.shape[1] * g.shape[2]) + g.shape[3:])


def fox_attention(q, k, v, r_q, r_k, q_offset):
    B, Tq, H, Dh = q.shape
    Tk = k.shape[1]
    blk = Q_BLOCK if Tq % Q_BLOCK == 0 else Tq
    nb = Tq // blk
    scale = Dh ** -0.5
    qb = jnp.moveaxis(q.reshape(B, nb, blk, H, Dh), 1, 0)
    rb = jnp.moveaxis(r_q.reshape(B, nb, blk, H), 1, 0)
    pos = (q_offset + jnp.arange(Tq, dtype=jnp.int32)).reshape(nb, blk)
    key_pos = jnp.arange(Tk, dtype=jnp.int32)
    rk = jnp.transpose(r_k, (0, 2, 1)).astype(F32)[:, :, None, :]

    def one_block(args):
        qi, ri, pi = args
        s = jnp.einsum('bqhd,bkhd->bhqk', qi, k).astype(F32) * scale
        s = s + rk - jnp.transpose(ri, (0, 2, 1)).astype(F32)[..., None]
        mask = key_pos[None, :] <= pi[:, None]
        s = jnp.where(mask[None, None], s, -jnp.inf)
        p = jax.nn.softmax(s, axis=-1).astype(v.dtype)
        return jnp.einsum('bhqk,bkhd->bqhd', p, v)

    out = lax.map(one_block, (qb, rb, pos))
    return jnp.moveaxis(out, 0, 1).reshape(B, Tq, H, Dh)


def short_conv(u, prev, w):
    T = u.shape[1]
    xp = jnp.concatenate([prev.astype(u.dtype), u], axis=1)
    y = sum(w[j] * xp[:, j:j + T] for j in range(CONV_K))
    return y, xp[:, -(CONV_K - 1):]


def mlstm_chunkwise(q, k, v, log_i, log_f, c0, n0, m0):
    B, T, NH, DK = q.shape
    DV = v.shape[-1]
    L = MLSTM_CHUNK if T % MLSTM_CHUNK == 0 else T
    nc = T // L

    def chunks4(a):
        return jnp.transpose(a.astype(F32).reshape(B, nc, L, NH, a.shape[-1]), (1, 0, 3, 2, 4))

    def chunks3(a):
        return jnp.transpose(a.astype(F32).reshape(B, nc, L, NH), (1, 0, 3, 2))

    xs = (chunks4(q), chunks4(k) * (DK ** -0.5), chunks4(v), chunks3(log_i), chunks3(log_f))
    causal = jnp.tril(jnp.ones((L, L), dtype=bool))

    def step(carry, inp):
        c, n, m = carry
        qc, kc, vc, ic, fc = inp
        b = jnp.cumsum(fc, axis=-1)
        dmat = jnp.where(causal, b[..., :, None] - b[..., None, :] + ic[..., None, :], -jnp.inf)
        inter = b + m[..., None]
        m_t = jnp.maximum(jnp.max(dmat, axis=-1), inter)
        w_intra = jnp.exp(dmat - m_t[..., None])
        w_inter = jnp.exp(inter - m_t)
        s = jnp.einsum('bhtd,bhsd->bhts', qc, kc) * w_intra
        num = jnp.einsum('bhts,bhsv->bhtv', s, vc) + w_inter[..., None] * jnp.einsum('bhvd,bhtd->bhtv', c, qc)
        den = jnp.sum(s, axis=-1) + w_inter * jnp.einsum('bhd,bhtd->bht', n, qc)
        h = num / jnp.maximum(jnp.abs(den), jnp.exp(-m_t))[..., None]
        g = b[..., -1]
        a = g[..., None] - b + ic
        m_new = jnp.maximum(g + m, jnp.max(a, axis=-1))
        decay = jnp.exp(g + m - m_new)
        wa = jnp.exp(a - m_new[..., None])
        c_new = decay[..., None, None] * c + jnp.einsum('bhs,bhsv,bhsd->bhvd', wa, vc, kc)
        n_new = decay[..., None] * n + jnp.einsum('bhs,bhsd->bhd', wa, kc)
        return (c_new, n_new, m_new), h

    (c1, n1, m1), hs = lax.scan(step, (c0.astype(F32), n0.astype(F32), m0.astype(F32)), xs)
    hs = jnp.transpose(hs, (1, 0, 3, 2, 4)).reshape(B, T, NH, DV)
    return hs, c1, n1, m1


def hier_moe(x, w_rg, b_rg, w_re, b_re, w_e_in, w_e_down):
    B, T, D = x.shape
    N = B * T
    xt = x.reshape(N, D)
    gl = (xt @ w_rg + b_rg).astype(F32)
    p_group = jax.nn.softmax(gl, axis=-1)
    g_star = jnp.argmax(gl, axis=-1)
    p_star = jnp.max(p_group, axis=-1)
    el = (xt @ w_re + b_re).astype(F32).reshape(N, N_GROUPS, EXPERTS_PER_GROUP)
    el_g = jnp.einsum('nge,ng->ne', el, jax.nn.one_hot(g_star, N_GROUPS, dtype=F32))
    top_v, top_i = lax.top_k(el_g, TOP_K_INNER)
    w_top = jax.nn.softmax(top_v, axis=-1) * p_star[:, None]
    expert_idx = g_star[:, None] * EXPERTS_PER_GROUP + top_i
    gates = jnp.sum(jax.nn.one_hot(expert_idx, N_EXPERTS, dtype=F32) * w_top[..., None], axis=1)
    blk = MOE_BLOCK if N % MOE_BLOCK == 0 else N

    def run(args):
        xb, gb = args
        hu = jnp.einsum('td,edf->tef', xb, w_e_in)
        hg, hv = jnp.split(hu, 2, axis=-1)
        a = jax.nn.silu(hg) * hv * gb[..., None].astype(xb.dtype)
        return jnp.einsum('tef,efd->td', a, w_e_down)

    out = lax.map(run, (xt.reshape(N // blk, blk, D), gates.reshape(N // blk, blk, N_EXPERTS)))
    return out.reshape(B, T, D)


def trunk_layer(x, p, attn_past, conv_prev, mlstm_state, q_offset):
    (w_in, b_in, conv_w, mlstm_norm_w, w_branch, w_out, ln1_g, ln1_b,
     w_rg, b_rg, w_re, b_re, w_e_in, w_e_down, ln2_g, ln2_b) = p
    B, T, _ = x.shape
    z = jnp.einsum('btd,dc->btc', x, w_in) + b_in
    (q_a, k_a, v_a, f_a, c_b, c_c, c_h, q_m, k_m, v_m, o_m, i_m, f_m,
     g_a, g_c, g_m) = jnp.split(z, _split_offsets(IN_SPLITS), axis=-1)

    q_a = q_a.reshape(B, T, ATTN_HEADS, ATTN_HEAD_DIM)
    k_a = k_a.reshape(B, T, ATTN_HEADS, ATTN_HEAD_DIM)
    v_a = v_a.reshape(B, T, ATTN_HEADS, ATTN_HEAD_DIM)
    logf_a = jax.nn.log_sigmoid(f_a.astype(F32))
    if attn_past is None:
        k_all, v_all, lf_all = k_a, v_a, logf_a
    else:
        k_past, v_past, lf_past = attn_past
        k_all = jnp.concatenate([k_past.astype(k_a.dtype), k_a], axis=1)
        v_all = jnp.concatenate([v_past.astype(v_a.dtype), v_a], axis=1)
        lf_all = jnp.concatenate([lf_past.astype(F32), logf_a], axis=1)
    r = lax.cumsum(lf_all, axis=1, reverse=True) - lf_all
    y_attn = fox_attention(q_a, k_all, v_all, r[:, -T:], r, q_offset).reshape(B, T, ATTN_WIDTH)

    conv_y, conv_new = short_conv(c_c * c_h, conv_prev, conv_w)
    y_conv = c_b * conv_y

    c0, n0, m0 = mlstm_state
    hq = q_m.reshape(B, T, MLSTM_HEADS, MLSTM_HEAD_DIM)
    hk = k_m.reshape(B, T, MLSTM_HEADS, MLSTM_HEAD_DIM)
    hv = v_m.reshape(B, T, MLSTM_HEADS, MLSTM_HEAD_DIM)
    h_t, c1, n1, m1 = mlstm_chunkwise(hq, hk, hv, i_m.astype(F32), jax.nn.log_sigmoid(f_m.astype(F32)), c0, n0, m0)
    o_gate = jax.nn.sigmoid(o_m.astype(F32)).reshape(B, T, MLSTM_HEADS, MLSTM_HEAD_DIM)
    y_mlstm = head_norm(o_gate * h_t, mlstm_norm_w).reshape(B, T, MLSTM_WIDTH).astype(x.dtype)

    w_ba, w_bc, w_bm = jnp.split(w_branch, [ATTN_WIDTH, ATTN_WIDTH + CONV_WIDTH], axis=0)
    merged = (jax.nn.sigmoid(g_a) * (y_attn @ w_ba)
              + jax.nn.sigmoid(g_c) * (y_conv @ w_bc)
              + jax.nn.sigmoid(g_m) * (y_mlstm @ w_bm))
    x1 = layer_norm(DEEPNORM_ALPHA * x + merged @ w_out, ln1_g, ln1_b)
    x2 = layer_norm(DEEPNORM_ALPHA * x1 + hier_moe(x1, w_rg, b_rg, w_re, b_re, w_e_in, w_e_down), ln2_g, ln2_b)
    return x2, (k_a, v_a, logf_a), conv_new, (c1, n1, m1)


def setup_inputs(seed: int = 0) -> dict:
    key = jax.random.key(seed)
    ks = jax.random.split(key, 32)
    n_pages = PAST_LEN // PAGE_SIZE
    n_pool = (5 * DEC_BATCH * n_pages + 3) // 4
    nrm = jax.random.normal
    x_prompt = nrm(ks[0], (BATCH, SEQ, D_MODEL), F32)
    x_sample = nrm(ks[1], (DEC_BATCH, DEC_SEQ, D_MODEL), F32)
    cache_k = nrm(ks[2], (DEPTH, n_pool, PAGE_SIZE, ATTN_HEADS, ATTN_HEAD_DIM), F32)
    cache_v = nrm(ks[3], (DEPTH, n_pool, PAGE_SIZE, ATTN_HEADS, ATTN_HEAD_DIM), F32) * DEEPNORM_BETA
    cache_logf = jax.nn.log_sigmoid(FORGET_BIAS_INIT + nrm(ks[4], (DEPTH, n_pool, PAGE_SIZE, ATTN_HEADS), F32))
    state_conv = 0.5 * nrm(ks[5], (DEPTH, DEC_BATCH, CONV_K - 1, CONV_WIDTH), F32)
    state_mlstm_c = 0.5 * nrm(ks[6], (DEPTH, DEC_BATCH, MLSTM_HEADS, MLSTM_HEAD_DIM, MLSTM_HEAD_DIM), F32)
    state_mlstm_n = nrm(ks[7], (DEPTH, DEC_BATCH, MLSTM_HEADS, MLSTM_HEAD_DIM), F32)
    state_mlstm_m = 0.5 * nrm(ks[8], (DEPTH, DEC_BATCH, MLSTM_HEADS), F32)
    page_table = jax.random.permutation(ks[9], n_pool)[:DEC_BATCH * n_pages].reshape(DEC_BATCH, n_pages).astype(jnp.int32)

    beta = DEEPNORM_BETA
    seg_scale = (1.0, 1.0, beta, 1.0, 1.0, 1.0, beta, 1.0, 1.0, beta, 1.0, 1.0, 1.0, 1.0, 1.0, 1.0)
    col_scale = jnp.asarray(np.concatenate([np.full((s,), c, np.float32) for s, c in zip(IN_SPLITS, seg_scale)]))
    seg_bias = (0.0, 0.0, 0.0, FORGET_BIAS_INIT, 0.0, 0.0, 0.0, 0.0, 0.0, 0.0, 0.0, 0.0, MLSTM_F_BIAS_INIT, 0.0, 0.0, 0.0)
    bias_offset = jnp.asarray(np.concatenate([np.full((s,), c, np.float32) for s, c in zip(IN_SPLITS, seg_bias)]))
    w_in = nrm(ks[10], (DEPTH, D_MODEL, IN_COLS), F32) * (D_MODEL ** -0.5) * col_scale
    b_in = 0.02 * nrm(ks[11], (DEPTH, IN_COLS), F32) + bias_offset
    conv_w = nrm(ks[12], (DEPTH, CONV_K, CONV_WIDTH), F32) * (CONV_K ** -0.5)
    mlstm_norm_w = 1.0 + 0.02 * nrm(ks[13], (DEPTH, MLSTM_WIDTH), F32)
    row_scale = jnp.asarray(np.concatenate([np.full((w,), w ** -0.5, np.float32) for w in (ATTN_WIDTH, CONV_WIDTH, MLSTM_WIDTH)]))
    w_branch = nrm(ks[14], (DEPTH, MIX_WIDTH, D_MODEL), F32) * row_scale[:, None] * beta
    w_out = nrm(ks[15], (DEPTH, D_MODEL, D_MODEL), F32) * (D_MODEL ** -0.5) * beta
    ln1_g = 1.0 + 0.02 * nrm(ks[16], (DEPTH, D_MODEL), F32)
    ln1_b = 0.02 * nrm(ks[17], (DEPTH, D_MODEL), F32)
    w_router_group = nrm(ks[18], (DEPTH, D_MODEL, N_GROUPS), F32) * (D_MODEL ** -0.5)
    b_router_group = 0.01 * nrm(ks[19], (DEPTH, N_GROUPS), F32)
    w_router_expert = nrm(ks[20], (DEPTH, D_MODEL, N_EXPERTS), F32) * (D_MODEL ** -0.5)
    b_router_expert = 0.01 * nrm(ks[21], (DEPTH, N_EXPERTS), F32)
    w_expert_in = nrm(ks[22], (DEPTH, N_EXPERTS, D_MODEL, 2 * D_EXPERT), F32) * (D_MODEL ** -0.5) * beta
    w_expert_down = nrm(ks[23], (DEPTH, N_EXPERTS, D_EXPERT, D_MODEL), F32) * (D_EXPERT ** -0.5) * beta
    ln2_g = 1.0 + 0.02 * nrm(ks[24], (DEPTH, D_MODEL), F32)
    ln2_b = 0.02 * nrm(ks[25], (DEPTH, D_MODEL), F32)
    return {'x_prompt': x_prompt, 'x_sample': x_sample,
            'cache_k': cache_k, 'cache_v': cache_v, 'cache_logf': cache_logf,
            'state_conv': state_conv, 'state_mlstm_c': state_mlstm_c,
            'state_mlstm_n': state_mlstm_n, 'state_mlstm_m': state_mlstm_m,
            'page_table': page_table,
            'w_in': w_in, 'b_in': b_in, 'conv_w': conv_w, 'mlstm_norm_w': mlstm_norm_w,
            'w_branch': w_branch, 'w_out': w_out, 'ln1_g': ln1_g, 'ln1_b': ln1_b,
            'w_router_group': w_router_group, 'b_router_group': b_router_group,
            'w_router_expert': w_router_expert, 'b_router_expert': b_router_expert,
            'w_expert_in': w_expert_in, 'w_expert_down': w_expert_down,
            'ln2_g': ln2_g, 'ln2_b': ln2_b}


def reference(x_prompt, x_sample, cache_k, cache_v, cache_logf, state_conv, state_mlstm_c,
              state_mlstm_n, state_mlstm_m, page_table, w_in, b_in, conv_w, mlstm_norm_w,
              w_branch, w_out, ln1_g, ln1_b, w_router_group, b_router_group, w_router_expert,
              b_router_expert, w_expert_in, w_expert_down, ln2_g, ln2_b):
    past_len = page_table.shape[1] * PAGE_SIZE
    xp, xs = x_prompt, x_sample
    bp = xp.shape[0]
    kp_l, vp_l, lfp_l, ks_l, vs_l, lfs_l = [], [], [], [], [], []
    cvp_l, cvs_l, cp_l, cs_l, np_l, ns_l, mp_l, ms_l = [], [], [], [], [], [], [], []
    for l in range(DEPTH):
        p = (w_in[l], b_in[l], conv_w[l], mlstm_norm_w[l], w_branch[l], w_out[l], ln1_g[l], ln1_b[l],
             w_router_group[l], b_router_group[l], w_router_expert[l], b_router_expert[l],
             w_expert_in[l], w_expert_down[l], ln2_g[l], ln2_b[l])
        conv0 = jnp.zeros((bp, CONV_K - 1, CONV_WIDTH), xp.dtype)
        ml0 = (jnp.zeros((bp, MLSTM_HEADS, MLSTM_HEAD_DIM, MLSTM_HEAD_DIM), F32),
               jnp.zeros((bp, MLSTM_HEADS, MLSTM_HEAD_DIM), F32),
               jnp.zeros((bp, MLSTM_HEADS), F32))
        xp, (k_n, v_n, lf_n), cv_n, (c_n, n_n, m_n) = trunk_layer(xp, p, None, conv0, ml0, 0)
        kp_l.append(k_n); vp_l.append(v_n); lfp_l.append(lf_n); cvp_l.append(cv_n)
        cp_l.append(c_n); np_l.append(n_n); mp_l.append(m_n)
        past = (gather_pages(cache_k[l], page_table), gather_pages(cache_v[l], page_table),
                gather_pages(cache_logf[l], page_table))
        xs, (k_n, v_n, lf_n), cv_n, (c_n, n_n, m_n) = trunk_layer(
            xs, p, past, state_conv[l], (state_mlstm_c[l], state_mlstm_n[l], state_mlstm_m[l]), past_len)
        ks_l.append(k_n); vs_l.append(v_n); lfs_l.append(lf_n); cvs_l.append(cv_n)
        cs_l.append(c_n); ns_l.append(n_n); ms_l.append(m_n)
    st = lambda a: jnp.stack(a, axis=0)
    return (xp, xs, st(kp_l), st(vp_l), st(lfp_l), st(ks_l), st(vs_l), st(lfs_l),
            st(cvp_l), st(cvs_l), st(cp_l), st(cs_l), st(np_l), st(ns_l), st(mp_l), st(ms_l))
```

```python
import functools
import math

import jax
import jax.numpy as jnp
from jax import lax
from jax.experimental import pallas as pl
from jax.experimental.pallas import tpu as pltpu

F32 = jnp.float32
BF16 = jnp.bfloat16
HIGHEST = lax.Precision.HIGHEST
NEG_INF = float("-inf")

LANES = 128
ROW_TILE = 256
V7X_VMEM_BYTES = 64 * 1024 * 1024
VMEM_RESERVE_BYTES = 8 * 1024 * 1024
LN_EPS = 1e-5
NT_DIMS = (((1,), (1,)), ((), ()))


def _vmem_limit(need_bytes):
    return int(min(V7X_VMEM_BYTES - VMEM_RESERVE_BYTES, max(need_bytes, 16 * 1024 * 1024)))


def _params(semantics, need_bytes):
    return pltpu.CompilerParams(dimension_semantics=semantics,
                                vmem_limit_bytes=_vmem_limit(need_bytes))


def _pick(n, prefs):
    for p in prefs:
        if n % p == 0:
            return p
    raise ValueError(f"no tile in {prefs} divides {n}")


def _log_sigmoid(x):
    return jnp.minimum(x, 0.0) - jnp.log1p(jnp.exp(-jnp.abs(x)))


def _sigmoid(x):
    return 1.0 / (1.0 + jnp.exp(-x))


def _cast_rows(src_ref, dst_ref, chunk):
    rows = src_ref.shape[0]

    def body(c, carry):
        r = pl.multiple_of(c * chunk, chunk)
        dst_ref[pl.ds(r, chunk), :] = src_ref[pl.ds(r, chunk), :].astype(BF16)
        return carry

    lax.fori_loop(0, rows // chunk, body, 0)


def _inproj_kernel(x_ref, w_ref, b_ref, o_ref, wb_ref, *, chunk):
    @pl.when(pl.program_id(1) == 0)
    def _():
        _cast_rows(w_ref, wb_ref, chunk)

    acc = jnp.dot(x_ref[...], wb_ref[...], preferred_element_type=F32)
    o_ref[...] = acc + b_ref[...]


def _inproj(xb, w, b, tn):
    n, k = xb.shape
    c = w.shape[1]
    tm = ROW_TILE
    need = 2 * k * tn * 4 + k * tn * 2 + 2 * tm * k * 2 + 3 * tm * tn * 4
    return pl.pallas_call(
        functools.partial(_inproj_kernel, chunk=_pick(k, (256, 128, 8))),
        grid=(c // tn, n // tm),
        in_specs=[pl.BlockSpec((tm, k), lambda j, i: (i, 0)),
                  pl.BlockSpec((k, tn), lambda j, i: (0, j)),
                  pl.BlockSpec((1, tn), lambda j, i: (0, j))],
        out_specs=pl.BlockSpec((tm, tn), lambda j, i: (i, j)),
        out_shape=jax.ShapeDtypeStruct((n, c), F32),
        scratch_shapes=[pltpu.VMEM((k, tn), BF16)],
        compiler_params=_params(("arbitrary", "arbitrary"), need + (4 << 20)),
        name="inproj",
    )(xb, w, b)


def _merge_kernel(ya_ref, yc_ref, ym_ref, ga_ref, gc_ref, gm_ref, w_ref, o_ref, wb_ref,
                  *, chunk, aw, cw):
    @pl.when(pl.program_id(1) == 0)
    def _():
        _cast_rows(w_ref, wb_ref, chunk)

    da = jnp.dot(ya_ref[...], wb_ref[0:aw, :], preferred_element_type=F32)
    dc = jnp.dot(yc_ref[...], wb_ref[aw:aw + cw, :], preferred_element_type=F32)
    dm = jnp.dot(ym_ref[...], wb_ref[aw + cw:, :], preferred_element_type=F32)
    out = _sigmoid(ga_ref[...]) * da + _sigmoid(gc_ref[...]) * dc + _sigmoid(gm_ref[...]) * dm
    o_ref[...] = out.astype(o_ref.dtype)


def _merge(ya, yc, ym, zc, g_col0, w_branch_l, layer, d):
    n = ya.shape[0]
    aw, cw, mw = ya.shape[1], yc.shape[1], ym.shape[1]
    mix = aw + cw + mw
    tm = ROW_TILE
    tn = _pick(d, (512, 256, 128))
    gb = g_col0 // tn
    nd = d // tn
    need = 2 * mix * tn * 4 + mix * tn * 2 + 2 * tm * mix * 2 + 8 * tm * tn * 4
    return pl.pallas_call(
        functools.partial(_merge_kernel, chunk=_pick(mix, (256, 128, 8)), aw=aw, cw=cw),
        grid=(nd, n // tm),
        in_specs=[pl.BlockSpec((tm, aw), lambda j, i: (i, 0)),
                  pl.BlockSpec((tm, cw), lambda j, i: (i, 0)),
                  pl.BlockSpec((tm, mw), lambda j, i: (i, 0)),
                  pl.BlockSpec((tm, tn), lambda j, i: (i, gb + j)),
                  pl.BlockSpec((tm, tn), lambda j, i: (i, gb + nd + j)),
                  pl.BlockSpec((tm, tn), lambda j, i: (i, gb + 2 * nd + j)),
                  pl.BlockSpec((None, mix, tn), lambda j, i: (layer, 0, j))],
        out_specs=pl.BlockSpec((tm, tn), lambda j, i: (i, j)),
        out_shape=jax.ShapeDtypeStruct((n, d), BF16),
        scratch_shapes=[pltpu.VMEM((mix, tn), BF16)],
        compiler_params=_params(("arbitrary", "arbitrary"), need + (4 << 20)),
        name="merge",
    )(ya, yc, ym, zc, zc, zc, w_branch_l)


def _outproj_kernel(m_ref, x_ref, w_ref, o_ref, wb_ref, *, chunk, alpha):
    @pl.when(pl.program_id(1) == 0)
    def _():
        _cast_rows(w_ref, wb_ref, chunk)

    acc = jnp.dot(m_ref[...], wb_ref[...], preferred_element_type=F32)
    o_ref[...] = alpha * x_ref[...] + acc


def _outproj(mb, x, w_out, layer, alpha):
    n, k = mb.shape
    d = w_out.shape[2]
    tm = ROW_TILE
    tn = _pick(d, (1024, 512, 256, 128))
    need = 2 * k * tn * 4 + k * tn * 2 + 2 * tm * k * 2 + 5 * tm * tn * 4
    return pl.pallas_call(
        functools.partial(_outproj_kernel, chunk=_pick(k, (256, 128, 8)), alpha=alpha),
        grid=(d // tn, n // tm),
        in_specs=[pl.BlockSpec((tm, k), lambda j, i: (i, 0)),
                  pl.BlockSpec((tm, tn), lambda j, i: (i, j)),
                  pl.BlockSpec((None, k, tn), lambda j, i: (layer, 0, j))],
        out_specs=pl.BlockSpec((tm, tn), lambda j, i: (i, j)),
        out_shape=jax.ShapeDtypeStruct((n, d), F32),
        scratch_shapes=[pltpu.VMEM((k, tn), BF16)],
        compiler_params=_params(("arbitrary", "arbitrary"), need + (4 << 20)),
        name="outproj",
    )(mb, x, w_out)


def _gates_kernel(zg_ref, lf_ref, cum_ref, cumt_ref, carry_ref):
    tt = zg_ref.shape[0]

    @pl.when(pl.program_id(1) == 0)
    def _():
        carry_ref[...] = jnp.zeros_like(carry_ref)

    lf = _log_sigmoid(zg_ref[...])
    row = lax.broadcasted_iota(jnp.int32, (tt, tt), 0)
    col = lax.broadcasted_iota(jnp.int32, (tt, tt), 1)
    tri = (row >= col).astype(F32)
    cum = jnp.dot(tri, lf, precision=HIGHEST, preferred_element_type=F32) + carry_ref[...]
    carry_ref[...] = cum[tt - 1:tt, :]
    lf_ref[...] = lf
    cum_ref[...] = cum
    cumt_ref[0] = cum.T


def _prompt_gates(zg, b, t):
    tt = _pick(t, (256, 128))
    nt = t // tt
    return pl.pallas_call(
        _gates_kernel,
        grid=(b, nt),
        in_specs=[pl.BlockSpec((tt, LANES), lambda bi, ti: (bi * nt + ti, 0))],
        out_specs=[pl.BlockSpec((tt, LANES), lambda bi, ti: (bi * nt + ti, 0)),
                   pl.BlockSpec((tt, LANES), lambda bi, ti: (bi * nt + ti, 0)),
                   pl.BlockSpec((1, LANES, tt), lambda bi, ti: (bi, 0, ti))],
        out_shape=[jax.ShapeDtypeStruct((b * t, LANES), F32),
                   jax.ShapeDtypeStruct((b * t, LANES), F32),
                   jax.ShapeDtypeStruct((b, LANES, t), F32)],
        scratch_shapes=[pltpu.VMEM((1, LANES), F32)],
        compiler_params=_params(("arbitrary", "arbitrary"), 0),
        name="prompt_gates",
    )(zg)


def _attn_kernel(q_ref, k_ref, v_ref, cq_ref, ck_ref, o_ref, m_ref, l_ref, acc_ref,
                 *, heads, hd, scale):
    qi = pl.program_id(1)
    ki = pl.program_id(2)
    tq = q_ref.shape[0]
    tk = k_ref.shape[0]

    @pl.when(ki == 0)
    def _():
        m_ref[...] = jnp.full(m_ref.shape, -1e30, F32)
        l_ref[...] = jnp.zeros_like(l_ref)
        acc_ref[...] = jnp.zeros_like(acc_ref)

    @pl.when(ki <= qi)
    def _():
        row = lax.broadcasted_iota(jnp.int32, (tq, tk), 0) + qi * tq
        col = lax.broadcasted_iota(jnp.int32, (tq, tk), 1) + ki * tk
        keep = col <= row
        cq = cq_ref[...]
        ck = ck_ref[0]
        for h in range(heads):
            sl = slice(h * hd, (h + 1) * hd)
            qh = q_ref[:, sl].astype(BF16)
            kh = k_ref[:, sl].astype(BF16)
            vh = v_ref[:, sl].astype(BF16)
            s = lax.dot_general(qh, kh, NT_DIMS, preferred_element_type=F32) * scale
            s = s + (cq[:, h:h + 1] - ck[h:h + 1, :])
            s = jnp.where(keep, s, -1e30)
            m_old = m_ref[h]
            m_new = jnp.maximum(m_old, jnp.max(s, axis=1, keepdims=True))
            p = jnp.exp(s - m_new)
            a = jnp.exp(m_old - m_new)
            l_ref[h] = a * l_ref[h] + jnp.sum(p, axis=1, keepdims=True)
            acc_ref[:, sl] = a * acc_ref[:, sl] + jnp.dot(p.astype(BF16), vh,
                                                          preferred_element_type=F32)
            m_ref[h] = m_new

    @pl.when(ki == qi)
    def _():
        for h in range(heads):
            sl = slice(h * hd, (h + 1) * hd)
            o_ref[:, sl] = (acc_ref[:, sl] / l_ref[h]).astype(o_ref.dtype)


def _prompt_attention(zc, cum, cumt, n_all, b, t, heads, hd):
    aw = heads * hd
    tq = _pick(t, (256, 128))
    nt = t // tq
    need = 2 * 3 * tq * aw * 4 + 2 * tq * aw * 2 + tq * aw * 4 + 2 * heads * tq * LANES * 4
    return pl.pallas_call(
        functools.partial(_attn_kernel, heads=heads, hd=hd, scale=hd ** -0.5),
        grid=(b, nt, nt),
        in_specs=[pl.BlockSpec((tq, aw), lambda bi, qi, ki: (bi * nt + qi, 0)),
                  pl.BlockSpec((tq, aw), lambda bi, qi, ki: (bi * nt + jnp.minimum(ki, qi), 1)),
                  pl.BlockSpec((tq, aw), lambda bi, qi, ki: (bi * nt + jnp.minimum(ki, qi), 2)),
                  pl.BlockSpec((tq, LANES), lambda bi, qi, ki: (bi * nt + qi, 0)),
                  pl.BlockSpec((1, LANES, tq), lambda bi, qi, ki: (bi, 0, jnp.minimum(ki, qi)))],
        out_specs=pl.BlockSpec((tq, aw), lambda bi, qi, ki: (bi * nt + qi, 0)),
        out_shape=jax.ShapeDtypeStruct((n_all, aw), BF16),
        scratch_shapes=[pltpu.VMEM((heads, tq, 1), F32),
                        pltpu.VMEM((heads, tq, 1), F32),
                        pltpu.VMEM((tq, aw), F32)],
        compiler_params=_params(("arbitrary", "arbitrary", "arbitrary"), need + (8 << 20)),
        name="prompt_attention",
    )(zc, zc, zc, cum, cumt)


def _decode_kernel(pt_ref, q_ref, kn_ref, vn_ref, lfn_ref, *rest, pps, heads, scale):
    del pt_ref
    k_refs = rest[:pps]
    v_refs = rest[pps:2 * pps]
    lf_refs = rest[2 * pps:3 * pps]
    o_ref = rest[3 * pps]
    m_ref, l_ref, acc_ref, carry_ref = rest[3 * pps + 1:]
    p_idx = pl.program_id(1)
    ps = lf_refs[0].shape[1]
    hd = q_ref.shape[2]
    q = q_ref[0]

    @pl.when(p_idx == 0)
    def _():
        m_ref[...] = jnp.sum(q * kn_ref[0], axis=1, keepdims=True) * scale
        l_ref[...] = jnp.ones_like(l_ref)
        acc_ref[...] = vn_ref[0]
        carry_ref[...] = lfn_ref[0]

    q8 = [jnp.broadcast_to(q[h:h + 1, :], (8, hd)).astype(BF16) for h in range(heads)]
    row = lax.broadcasted_iota(jnp.int32, (ps, ps), 0)
    col = lax.broadcasted_iota(jnp.int32, (ps, ps), 1)
    later = (row > col).astype(F32)
    for i in range(pps):
        lf = lf_refs[i][...]
        bias = jnp.dot(lf, later, precision=HIGHEST, preferred_element_type=F32) + carry_ref[...]
        carry_ref[...] = carry_ref[...] + jnp.sum(lf, axis=1, keepdims=True)
        rows = []
        for h in range(heads):
            kh = k_refs[i][pl.ds(h, ps, stride=heads), :].astype(BF16)
            rows.append(lax.dot_general(q8[h], kh, NT_DIMS, preferred_element_type=F32)[0:1, :])
        s = jnp.concatenate(rows, axis=0) * scale + bias
        m_old = m_ref[...]
        m_new = jnp.maximum(m_old, jnp.max(s, axis=1, keepdims=True))
        pr = jnp.exp(s - m_new)
        a = jnp.exp(m_old - m_new)
        l_ref[...] = a * l_ref[...] + jnp.sum(pr, axis=1, keepdims=True)
        outs = []
        for h in range(heads):
            ph = jnp.broadcast_to(pr[h:h + 1, :], (8, ps)).astype(BF16)
            vh = v_refs[i][pl.ds(h, ps, stride=heads), :].astype(BF16)
            outs.append(jnp.dot(ph, vh, preferred_element_type=F32)[0:1, :])
        acc_ref[...] = a * acc_ref[...] + jnp.concatenate(outs, axis=0)
        m_ref[...] = m_new

    @pl.when(p_idx == pl.num_programs(1) - 1)
    def _():
        o_ref[0] = acc_ref[...] / l_ref[...]


def _decode_attention(page_table, q, k_new, v_new, lf_new, cache_k2, cache_v2, cache_lft, layer):
    db, heads, hd = q.shape
    n_pages = page_table.shape[1]
    ps = cache_lft.shape[3]
    pps = _pick(n_pages, (4, 2, 1))
    steps = n_pages // pps
    pt_flat = page_table.reshape(-1)

    def page_map(i):
        def index_map(b, p, pt):
            return (layer, pt[b * n_pages + (n_pages - 1 - (p * pps + i))], 0, 0)
        return index_map

    small = lambda b, p, pt: (b, 0, 0)
    in_specs = [pl.BlockSpec((1, heads, hd), small), pl.BlockSpec((1, heads, hd), small),
                pl.BlockSpec((1, heads, hd), small), pl.BlockSpec((1, heads, 1), small)]
    in_specs += [pl.BlockSpec((None, None, ps * heads, hd), page_map(i)) for i in range(pps)]
    in_specs += [pl.BlockSpec((None, None, ps * heads, hd), page_map(i)) for i in range(pps)]
    in_specs += [pl.BlockSpec((None, None, heads, ps), page_map(i)) for i in range(pps)]
    need = 2 * 2 * pps * ps * heads * hd * 4
    grid_spec = pltpu.PrefetchScalarGridSpec(
        num_scalar_prefetch=1,
        grid=(db, steps),
        in_specs=in_specs,
        out_specs=pl.BlockSpec((1, heads, hd), small),
        scratch_shapes=[pltpu.VMEM((heads, 1), F32), pltpu.VMEM((heads, 1), F32),
                        pltpu.VMEM((heads, hd), F32), pltpu.VMEM((heads, 1), F32)],
    )
    return pl.pallas_call(
        functools.partial(_decode_kernel, pps=pps, heads=heads, scale=hd ** -0.5),
        grid_spec=grid_spec,
        out_shape=jax.ShapeDtypeStruct((db, heads, hd), F32),
        compiler_params=_params(("arbitrary", "arbitrary"), need + (8 << 20)),
        name="decode_attention",
    )(pt_flat, q, k_new, v_new, lf_new,
      *([cache_k2] * pps), *([cache_v2] * pps), *([cache_lft] * pps))


def _conv_kernel(cb_ref, cc_ref, ch_ref, w_ref, y_ref, cn_ref):
    t = cc_ref.shape[0]
    u = cc_ref[...] * ch_ref[...]
    row = lax.broadcasted_iota(jnp.int32, u.shape, 0)
    u1 = jnp.where(row >= 1, pltpu.roll(u, 1, axis=0), 0.0)
    u2 = jnp.where(row >= 2, pltpu.roll(u, 2, axis=0), 0.0)
    w = w_ref[...]
    y = cb_ref[...] * (w[0:1, :] * u2 + w[1:2, :] * u1 + w[2:3, :] * u)
    y_ref[...] = y.astype(y_ref.dtype)
    cn_ref[0] = u[t - 2:t, :]


def _prompt_conv(zc, conv_w, layer, col0, n_all, b, t, cw):
    tc = _pick(cw, (256, 128))
    cb0 = col0 // tc
    nc = cw // tc
    kk = conv_w.shape[1]
    need = 2 * 3 * t * tc * 4 + 2 * t * tc * 2 + 6 * t * tc * 4
    return pl.pallas_call(
        _conv_kernel,
        grid=(b, nc),
        in_specs=[pl.BlockSpec((t, tc), lambda bi, j: (bi, cb0 + j)),
                  pl.BlockSpec((t, tc), lambda bi, j: (bi, cb0 + nc + j)),
                  pl.BlockSpec((t, tc), lambda bi, j: (bi, cb0 + 2 * nc + j)),
                  pl.BlockSpec((None, kk, tc), lambda bi, j: (layer, 0, j))],
        out_specs=[pl.BlockSpec((t, tc), lambda bi, j: (bi, j)),
                   pl.BlockSpec((1, kk - 1, tc), lambda bi, j: (bi, 0, j))],
        out_shape=[jax.ShapeDtypeStruct((n_all, cw), BF16),
                   jax.ShapeDtypeStruct((b, kk - 1, cw), F32)],
        compiler_params=_params(("arbitrary", "arbitrary"), need + (4 << 20)),
        name="prompt_conv",
    )(zc, zc, zc, conv_w)


def _head_norm(x, gain_row):
    mu = jnp.mean(x, axis=1, keepdims=True)
    xc = x - mu
    var = jnp.mean(xc * xc, axis=1, keepdims=True)
    return xc * lax.rsqrt(var + LN_EPS) * gain_row


def _mlstm_kernel(q_ref, k_ref, v_ref, o_ref, zg_ref, gain_ref, y_ref, c_out, n_out, m_out,
                  c_ref, n_ref, m_ref, *, heads, md, i0, f0):
    ci = pl.program_id(1)
    lc = q_ref.shape[0]

    @pl.when(ci == 0)
    def _():
        c_ref[...] = jnp.zeros_like(c_ref)
        n_ref[...] = jnp.zeros_like(n_ref)
        m_ref[...] = jnp.zeros_like(m_ref)

    zg = zg_ref[...]
    zgt = zg.T
    row = lax.broadcasted_iota(jnp.int32, (lc, lc), 0)
    col = lax.broadcasted_iota(jnp.int32, (lc, lc), 1)
    causal = col <= row
    bcols = jnp.dot(causal.astype(F32), _log_sigmoid(zg), precision=HIGHEST,
                    preferred_element_type=F32)
    brows = bcols.T
    kscale = md ** -0.5
    for h in range(heads):
        sl = slice(h * md, (h + 1) * md)
        bcol = bcols[:, f0 + h:f0 + h + 1]
        brow = brows[f0 + h:f0 + h + 1, :]
        icol = zg[:, i0 + h:i0 + h + 1]
        irow = zgt[i0 + h:i0 + h + 1, :]
        m_prev = m_ref[h]
        cst = c_ref[h]
        nst = n_ref[h]
        qh = q_ref[:, sl]
        kh = k_ref[:, sl] * kscale
        qb = qh.astype(BF16)
        kb = kh.astype(BF16)
        vb = v_ref[:, sl].astype(BF16)

        dmat = jnp.where(causal, bcol - brow + irow, NEG_INF)
        inter = bcol + m_prev
        m_t = jnp.maximum(jnp.max(dmat, axis=1, keepdims=True), inter)
        w_intra = jnp.exp(dmat - m_t)
        w_inter = jnp.exp(inter - m_t)
        s = lax.dot_general(qb, kb, NT_DIMS, preferred_element_type=F32) * w_intra
        num = jnp.dot(s.astype(BF16), vb, preferred_element_type=F32) + w_inter * lax.dot_general(
            qb, cst.astype(BF16), NT_DIMS, preferred_element_type=F32)
        den = jnp.sum(s, axis=1, keepdims=True) + w_inter * jnp.sum(qh * nst, axis=1, keepdims=True)
        hh = num / jnp.maximum(jnp.abs(den), jnp.exp(-m_t))
        og = _sigmoid(o_ref[:, sl])
        y_ref[:, sl] = _head_norm(og * hh, gain_ref[:, sl]).astype(y_ref.dtype)

        g = bcol[lc - 1:lc, :]
        acol = g - bcol + icol
        m_new = jnp.maximum(g + m_prev, jnp.max(acol, axis=0, keepdims=True))
        decay = jnp.exp(g + m_prev - m_new)
        wa = jnp.exp(acol - m_new)
        vwt = (v_ref[:, sl] * wa).T.astype(BF16)
        c_ref[h] = decay * cst + jnp.dot(vwt, kb, preferred_element_type=F32)
        n_ref[h] = decay * nst + jnp.sum(kh * wa, axis=0, keepdims=True)
        m_ref[h] = m_new

    @pl.when(ci == pl.num_programs(1) - 1)
    def _():
        c_out[0] = c_ref[...]
        n_out[0] = n_ref[...]
        m_out[0] = m_ref[...]


def _prompt_mlstm(zc, zg, gain, layer, col0, n_all, b, t, heads, md, i0, f0):
    mw = heads * md
    lc = _pick(t, (256, 128))
    nch = t // lc
    qb0 = col0 // mw
    need = 2 * 4 * lc * mw * 4 + 2 * lc * mw * 2 + 3 * heads * md * md * 4 + 12 * lc * lc * 4
    zblock = lambda off: pl.BlockSpec((lc, mw), lambda bi, ci: (bi * nch + ci, qb0 + off))
    return pl.pallas_call(
        functools.partial(_mlstm_kernel, heads=heads, md=md, i0=i0, f0=f0),
        grid=(b, nch),
        in_specs=[zblock(0), zblock(1), zblock(2), zblock(3),
                  pl.BlockSpec((lc, LANES), lambda bi, ci: (bi * nch + ci, 0)),
                  pl.BlockSpec((None, 1, mw), lambda bi, ci: (layer, 0, 0))],
        out_specs=[pl.BlockSpec((lc, mw), lambda bi, ci: (bi * nch + ci, 0)),
                   pl.BlockSpec((1, heads, md, md), lambda bi, ci: (bi, 0, 0, 0)),
                   pl.BlockSpec((1, heads, 1, md), lambda bi, ci: (bi, 0, 0, 0)),
                   pl.BlockSpec((1, heads, 1, 1), lambda bi, ci: (bi, 0, 0, 0))],
        out_shape=[jax.ShapeDtypeStruct((n_all, mw), BF16),
                   jax.ShapeDtypeStruct((b, heads, md, md), F32),
                   jax.ShapeDtypeStruct((b, heads, 1, md), F32),
                   jax.ShapeDtypeStruct((b, heads, 1, 1), F32)],
        scratch_shapes=[pltpu.VMEM((heads, md, md), F32),
                        pltpu.VMEM((heads, 1, md), F32),
                        pltpu.VMEM((heads, 1, 1), F32)],
        compiler_params=_params(("arbitrary", "arbitrary"), need + (8 << 20)),
        name="prompt_mlstm",
    )(zc, zc, zc, zc, zg, gain)


def _sample_mix_kernel(cb_ref, cc_ref, ch_ref, cw_ref, prev_ref, q_ref, k_ref, v_ref, vcol_ref,
                       o_ref, zg_ref, gain_ref, c0_ref, n0_ref, m0_ref,
                       yc_ref, cn_ref, ym_ref, c1_ref, n1_ref, m1_ref, lf_ref,
                       *, heads, md, i0, f0):
    u = cc_ref[0] * ch_ref[0]
    prev = prev_ref[0]
    w = cw_ref[...]
    yc_ref[0] = cb_ref[0] * (w[0:1, :] * prev[0:1, :] + w[1:2, :] * prev[1:2, :] + w[2:3, :] * u)
    cn_ref[0, 0:1, :] = prev[1:2, :]
    cn_ref[0, 1:2, :] = u

    zg = zg_ref[0]
    lfz = _log_sigmoid(zg)
    lf_ref[0] = lfz
    lane = lax.broadcasted_iota(jnp.int32, (1, heads), 1)
    m_row = jnp.zeros((1, heads), F32)
    kscale = md ** -0.5
    for h in range(heads):
        sl = slice(h * md, (h + 1) * md)
        qh = q_ref[0][:, sl]
        kh = k_ref[0][:, sl] * kscale
        vh = v_ref[0][:, sl]
        ig = zg[:, i0 + h:i0 + h + 1]
        fg = lfz[:, f0 + h:f0 + h + 1]
        m_prev = m0_ref[0][:, h:h + 1]
        cst = c0_ref[0, h]
        nst = n0_ref[0][h:h + 1, :]
        inter = fg + m_prev
        m_t = jnp.maximum(ig, inter)
        w_intra = jnp.exp(ig - m_t)
        w_inter = jnp.exp(inter - m_t)
        s = jnp.sum(qh * kh, axis=1, keepdims=True) * w_intra
        q8 = jnp.broadcast_to(qh, (8, md))
        cq = lax.dot_general(q8, cst, NT_DIMS, precision=HIGHEST,
                             preferred_element_type=F32)[0:1, :]
        num = s * vh + w_inter * cq
        den = s + w_inter * jnp.sum(nst * qh, axis=1, keepdims=True)
        hh = num / jnp.maximum(jnp.abs(den), jnp.exp(-m_t))
        og = _sigmoid(o_ref[0][:, sl])
        ym_ref[0, :, sl] = _head_norm(og * hh, gain_ref[:, sl])
        m_new = jnp.maximum(inter, ig)
        decay = jnp.exp(inter - m_new)
        wa = jnp.exp(ig - m_new)
        c1_ref[0, h] = decay * cst + (vcol_ref[0, h] * wa) * kh
        n1_ref[0, h:h + 1, :] = decay * nst + wa * kh
        m_row = jnp.where(lane == h, m_new, m_row)
    m1_ref[0] = m_row


def _sample_mix(zs, zgs, conv_w, gain, state_conv, c0, n0, m0, layer, cols, heads, md, i0, f0):
    db = zs.shape[0]
    cw = state_conv.shape[-1]
    mw = heads * md
    kk = conv_w.shape[1]
    cb0, q0 = cols
    seg = lambda a, width: zs[:, a:a + width].reshape(db, 1, width)
    cb, cc, ch = seg(cb0, cw), seg(cb0 + cw, cw), seg(cb0 + 2 * cw, cw)
    q, k, v, o = (seg(q0 + i * mw, mw) for i in range(4))
    vcol = v.reshape(db, heads, md, 1)
    zg3 = zgs.reshape(db, 1, LANES)
    m03 = m0.reshape(m0.shape[0], db, 1, heads)
    row3 = lambda width: pl.BlockSpec((1, 1, width), lambda b: (b, 0, 0))
    in_specs = [row3(cw), row3(cw), row3(cw),
                pl.BlockSpec((None, kk, cw), lambda b: (layer, 0, 0)),
                pl.BlockSpec((None, 1, kk - 1, cw), lambda b: (layer, b, 0, 0)),
                row3(mw), row3(mw), row3(mw),
                pl.BlockSpec((1, heads, md, 1), lambda b: (b, 0, 0, 0)),
                row3(mw), row3(LANES),
                pl.BlockSpec((None, 1, mw), lambda b: (layer, 0, 0)),
                pl.BlockSpec((None, 1, heads, md, md), lambda b: (layer, b, 0, 0, 0)),
                pl.BlockSpec((None, 1, heads, md), lambda b: (layer, b, 0, 0)),
                pl.BlockSpec((None, 1, 1, heads), lambda b: (layer, b, 0, 0))]
    out_specs = [row3(cw),
                 pl.BlockSpec((1, kk - 1, cw), lambda b: (b, 0, 0)),
                 row3(mw),
                 pl.BlockSpec((1, heads, md, md), lambda b: (b, 0, 0, 0)),
                 pl.BlockSpec((1, heads, md), lambda b: (b, 0, 0)),
                 pl.BlockSpec((1, 1, heads), lambda b: (b, 0, 0)),
                 row3(LANES)]
    out_shape = [jax.ShapeDtypeStruct((db, 1, cw), F32),
                 jax.ShapeDtypeStruct((db, kk - 1, cw), F32),
                 jax.ShapeDtypeStruct((db, 1, mw), F32),
                 jax.ShapeDtypeStruct((db, heads, md, md), F32),
                 jax.ShapeDtypeStruct((db, heads, md), F32),
                 jax.ShapeDtypeStruct((db, 1, heads), F32),
                 jax.ShapeDtypeStruct((db, 1, LANES), F32)]
    need = 4 * heads * md * md * 4 + 2 * heads * md * LANES * 4
    return pl.pallas_call(
        functools.partial(_sample_mix_kernel, heads=heads, md=md, i0=i0, f0=f0),
        grid=(db,),
        in_specs=in_specs,
        out_specs=out_specs,
        out_shape=out_shape,
        compiler_params=_params(("arbitrary",), need + (8 << 20)),
        name="sample_mix",
    )(cb, cc, ch, conv_w, state_conv, q, k, v, vcol, o, zg3, gain, c0, n0, m03)


def _put_rows_kernel(dst_ref, rows_ref, o_ref):
    del dst_ref
    o_ref[...] = rows_ref[...]


def _put_tail_rows(dst, rows):
    n, w = dst.shape
    last = n // ROW_TILE - 1
    return pl.pallas_call(
        _put_rows_kernel,
        grid=(1,),
        in_specs=[pl.BlockSpec(memory_space=pl.ANY),
                  pl.BlockSpec((ROW_TILE, w), lambda i: (0, 0))],
        out_specs=pl.BlockSpec((ROW_TILE, w), lambda i: (last, 0)),
        out_shape=jax.ShapeDtypeStruct(dst.shape, dst.dtype),
        input_output_aliases={0: 0},
        name="put_tail_rows",
    )(dst, rows)


def _layer_norm_rows(y, g, b):
    mu = jnp.mean(y, axis=1, keepdims=True)
    yc = y - mu
    var = jnp.mean(yc * yc, axis=1, keepdims=True)
    return yc * lax.rsqrt(var + LN_EPS) * g + b


def _first_argmax(x, width):
    lane = lax.broadcasted_iota(jnp.int32, x.shape, 1).astype(F32)
    mx = jnp.max(x, axis=1, keepdims=True)
    idx = jnp.min(jnp.where(x == mx, lane, float(width)), axis=1, keepdims=True)
    return mx, idx


def _ln_router_kernel(y_ref, g_ref, b_ref, wr_ref, br_ref, x_ref, xb_ref, r_ref, *, groups, epg):
    x1 = _layer_norm_rows(y_ref[...], g_ref[...], b_ref[...])
    x_ref[...] = x1
    xb_ref[...] = x1.astype(BF16)
    logits = jnp.dot(x1, wr_ref[...], precision=HIGHEST, preferred_element_type=F32) + br_ref[...]
    gl = logits[:, 0:groups]
    gmax, gstar = _first_argmax(gl, groups)
    p_star = 1.0 / jnp.sum(jnp.exp(gl - gmax), axis=1, keepdims=True)
    elg = logits[:, groups:groups + epg]
    for g in range(1, groups):
        elg = jnp.where(gstar == g, logits[:, groups + g * epg:groups + (g + 1) * epg], elg)
    v0, i0 = _first_argmax(elg, epg)
    lane = lax.broadcasted_iota(jnp.int32, elg.shape, 1).astype(F32)
    v1, i1 = _first_argmax(jnp.where(lane == i0, NEG_INF, elg), epg)
    e1 = jnp.exp(v1 - v0)
    w0 = p_star / (1.0 + e1)
    w1 = p_star * e1 / (1.0 + e1)
    out_lane = lax.broadcasted_iota(jnp.int32, r_ref.shape, 1)
    ex0 = (gstar * epg + i0).astype(F32)
    ex1 = (gstar * epg + i1).astype(F32)
    r = jnp.where(out_lane == 0, ex0, 0.0)
    r = jnp.where(out_lane == 1, ex1, r)
    r = jnp.where(out_lane == 2, w0, r)
    r = jnp.where(out_lane == 3, w1, r)
    r_ref[...] = r


def _ln_router(y, g, b, wr, br, layer, groups, epg):
    n, d = y.shape
    tm = ROW_TILE
    need = 2 * tm * d * 4 * 2 + 2 * tm * d * 2 + 2 * d * LANES * 4 + 6 * tm * d * 4
    vec = pl.BlockSpec((None, 1, d), lambda i: (layer, 0, 0))
    return pl.pallas_call(
        functools.partial(_ln_router_kernel, groups=groups, epg=epg),
        grid=(n // tm,),
        in_specs=[pl.BlockSpec((tm, d), lambda i: (i, 0)), vec, vec,
                  pl.BlockSpec((d, LANES), lambda i: (0, 0)),
                  pl.BlockSpec((1, LANES), lambda i: (0, 0))],
        out_specs=[pl.BlockSpec((tm, d), lambda i: (i, 0)),
                   pl.BlockSpec((tm, d), lambda i: (i, 0)),
                   pl.BlockSpec((tm, LANES), lambda i: (i, 0))],
        out_shape=[jax.ShapeDtypeStruct((n, d), F32),
                   jax.ShapeDtypeStruct((n, d), BF16),
                   jax.ShapeDtypeStruct((n, LANES), F32)],
        compiler_params=_params(("arbitrary",), need + (4 << 20)),
        name="ln_router",
    )(y, g, b, wr, br)


def _moe_in_kernel(te_ref, nv_ref, x_ref, wg_ref, wv_ref, a_ref, wgb_ref, wvb_ref, *, chunk):
    t = pl.program_id(1)
    prev = te_ref[jnp.maximum(t - 1, 0)]
    fresh = jnp.logical_or(t == 0, te_ref[t] != prev)

    @pl.when(jnp.logical_and(fresh, t < nv_ref[0]))
    def _():
        _cast_rows(wg_ref, wgb_ref, chunk)
        _cast_rows(wv_ref, wvb_ref, chunk)

    @pl.when(t < nv_ref[0])
    def _():
        x = x_ref[...]
        hg = jnp.dot(x, wgb_ref[...], preferred_element_type=F32)
        hv = jnp.dot(x, wvb_ref[...], preferred_element_type=F32)
        a_ref[...] = (hg * _sigmoid(hg) * hv).astype(a_ref.dtype)


def _moe_in(tile_expert, n_valid, xs, w_e_in, layer, de, tm):
    r, d = xs.shape
    tf = _pick(de, (512, 256, 128))
    nf = de // tf
    n_tiles = r // tm
    need = 2 * 2 * d * tf * 4 + 2 * d * tf * 2 + 2 * tm * d * 2 + 6 * tm * tf * 4
    row = lambda j, t, te, nv: (jnp.minimum(t, nv[0] - 1), 0)
    grid_spec = pltpu.PrefetchScalarGridSpec(
        num_scalar_prefetch=2,
        grid=(nf, n_tiles),
        in_specs=[pl.BlockSpec((tm, d), row),
                  pl.BlockSpec((None, None, d, tf), lambda j, t, te, nv: (layer, te[t], 0, j)),
                  pl.BlockSpec((None, None, d, tf), lambda j, t, te, nv: (layer, te[t], 0, nf + j))],
        out_specs=pl.BlockSpec((tm, tf), lambda j, t, te, nv: (jnp.minimum(t, nv[0] - 1), j)),
        scratch_shapes=[pltpu.VMEM((d, tf), BF16), pltpu.VMEM((d, tf), BF16)],
    )
    return pl.pallas_call(
        functools.partial(_moe_in_kernel, chunk=_pick(d, (256, 128, 8))),
        grid_spec=grid_spec,
        out_shape=jax.ShapeDtypeStruct((r, de), BF16),
        compiler_params=_params(("arbitrary", "arbitrary"), need + (4 << 20)),
        name="moe_in",
    )(tile_expert, n_valid, xs, w_e_in, w_e_in)


def _moe_down_kernel(te_ref, nv_ref, a_ref, w_ref, y_ref, wb_ref, *, chunk):
    t = pl.program_id(1)
    prev = te_ref[jnp.maximum(t - 1, 0)]
    fresh = jnp.logical_or(t == 0, te_ref[t] != prev)

    @pl.when(jnp.logical_and(fresh, t < nv_ref[0]))
    def _():
        _cast_rows(w_ref, wb_ref, chunk)

    @pl.when(t < nv_ref[0])
    def _():
        y_ref[...] = jnp.dot(a_ref[...], wb_ref[...], preferred_element_type=F32)


def _moe_down(tile_expert, n_valid, a, w_e_down, layer, tm):
    r, de = a.shape
    d = w_e_down.shape[3]
    tn = _pick(d, (2048, 1024, 512, 256, 128))
    n_tiles = r // tm
    need = 2 * de * tn * 4 + de * tn * 2 + 2 * tm * de * 2 + 3 * tm * tn * 4
    row = lambda j, t, te, nv: (jnp.minimum(t, nv[0] - 1), 0)
    grid_spec = pltpu.PrefetchScalarGridSpec(
        num_scalar_prefetch=2,
        grid=(d // tn, n_tiles),
        in_specs=[pl.BlockSpec((tm, de), row),
                  pl.BlockSpec((None, None, de, tn), lambda j, t, te, nv: (layer, te[t], 0, j))],
        out_specs=pl.BlockSpec((tm, tn), lambda j, t, te, nv: (jnp.minimum(t, nv[0] - 1), j)),
        scratch_shapes=[pltpu.VMEM((de, tn), BF16)],
    )
    return pl.pallas_call(
        functools.partial(_moe_down_kernel, chunk=_pick(de, (256, 128, 8))),
        grid_spec=grid_spec,
        out_shape=jax.ShapeDtypeStruct((r, d), F32),
        compiler_params=_params(("arbitrary", "arbitrary"), need + (4 << 20)),
        name="moe_down",
    )(tile_expert, n_valid, a, w_e_down)


def _combine_ln_kernel(x_ref, y0_ref, y1_ref, r_ref, g_ref, b_ref, o_ref, ob_ref, *, alpha):
    r = r_ref[...]
    y = alpha * x_ref[...] + r[:, 2:3] * y0_ref[...] + r[:, 3:4] * y1_ref[...]
    x2 = _layer_norm_rows(y, g_ref[...], b_ref[...])
    o_ref[...] = x2
    ob_ref[...] = x2.astype(BF16)


def _combine_ln(x1, y0, y1, route, g, b, layer, alpha):
    n, d = x1.shape
    tm = ROW_TILE
    need = 2 * 4 * tm * d * 4 + 2 * tm * d * 2 + 4 * tm * d * 4
    blk = pl.BlockSpec((tm, d), lambda i: (i, 0))
    vec = pl.BlockSpec((None, 1, d), lambda i: (layer, 0, 0))
    return pl.pallas_call(
        functools.partial(_combine_ln_kernel, alpha=alpha),
        grid=(n // tm,),
        in_specs=[blk, blk, blk, pl.BlockSpec((tm, LANES), lambda i: (i, 0)), vec, vec],
        out_specs=[blk, blk],
        out_shape=[jax.ShapeDtypeStruct((n, d), F32), jax.ShapeDtypeStruct((n, d), BF16)],
        compiler_params=_params(("arbitrary",), need + (4 << 20)),
        name="combine_ln",
    )(x1, y0, y1, route, g, b)


def _expert_rows(route, n_tok, n_experts, tm):
    e = route[:n_tok, 0:2].astype(jnp.int32).reshape(-1)
    n_pairs = e.shape[0]
    onehot = (e[:, None] == jnp.arange(n_experts, dtype=jnp.int32)[None, :]).astype(jnp.int32)
    running = jnp.cumsum(onehot, axis=0)
    counts = running[-1]
    rank = jnp.sum(running * onehot, axis=1) - 1
    padded = (counts + tm - 1) // tm * tm
    ends = jnp.cumsum(padded)
    starts = ends - padded
    pos = jnp.sum(onehot * starts[None, :], axis=1) + rank
    n_rows = (n_pairs + n_experts * (tm - 1)) // tm * tm
    n_tiles = n_rows // tm
    tok = jnp.arange(n_pairs, dtype=jnp.int32) // 2
    row_token = jnp.zeros((n_rows,), jnp.int32).at[pos].set(tok)
    n_valid = (ends[-1] // tm).astype(jnp.int32)
    tile_start = jnp.arange(n_tiles, dtype=jnp.int32) * tm
    tile_expert = jnp.sum((tile_start[:, None] >= ends[None, :]).astype(jnp.int32), axis=1)
    last_expert = jnp.max(jnp.where(counts > 0, jnp.arange(n_experts, dtype=jnp.int32), 0))
    tile_expert = jnp.minimum(tile_expert, last_expert).astype(jnp.int32)
    return row_token, pos.reshape(n_tok, 2), tile_expert, n_valid.reshape(1)


def kernel(x_prompt, x_sample, cache_k, cache_v, cache_logf, state_conv, state_mlstm_c,
           state_mlstm_n, state_mlstm_m, page_table, w_in, b_in, conv_w, mlstm_norm_w,
           w_branch, w_out, ln1_g, ln1_b, w_router_group, b_router_group, w_router_expert,
           b_router_expert, w_expert_in, w_expert_down, ln2_g, ln2_b):
    bp, t, d = x_prompt.shape
    db = x_sample.shape[0]
    depth = w_in.shape[0]
    n_pool, ps, heads, hd = cache_k.shape[1:]
    aw = heads * hd
    cw = state_conv.shape[-1]
    mh, md = state_mlstm_c.shape[2], state_mlstm_c.shape[3]
    mw = mh * md
    groups = w_router_group.shape[-1]
    n_experts = w_router_expert.shape[-1]
    epg = n_experts // groups
    de = w_expert_down.shape[2]
    alpha = float((2 * depth) ** 0.25)
    n_p = bp * t
    n_tok = n_p + db
    n_all = n_p + ROW_TILE
    moe_tm = ROW_TILE

    c_f = 3 * aw
    c_conv = c_f + heads
    c_m = c_conv + 3 * cw
    c_i = c_m + 4 * mw
    c_g = c_i + 2 * mh
    zc_conv = 3 * aw
    zc_m = zc_conv + 3 * cw
    zc_g = zc_m + 4 * mw
    zc_cols = zc_g + 3 * d
    i0, f0 = heads, heads + mh
    assert heads + 2 * mh <= LANES and groups + n_experts <= LANES
    tn_in = _pick(zc_cols, (1024, 512, 256, 128))

    pad_rows = jnp.zeros((ROW_TILE - db, d), F32)
    x_all = jnp.concatenate([x_prompt.reshape(n_p, d), x_sample.reshape(db, d), pad_rows], axis=0)
    xb_all = x_all.astype(BF16)

    cache_k2 = cache_k.reshape(depth, n_pool, ps * heads, hd)
    cache_v2 = cache_v.reshape(depth, n_pool, ps * heads, hd)
    cache_lft = jnp.swapaxes(cache_logf, 2, 3)
    gain3 = mlstm_norm_w.reshape(depth, 1, mw)
    ln1_g3, ln1_b3 = ln1_g.reshape(depth, 1, d), ln1_b.reshape(depth, 1, d)
    ln2_g3, ln2_b3 = ln2_g.reshape(depth, 1, d), ln2_b.reshape(depth, 1, d)

    def tail(rows):
        rows = rows.astype(BF16)
        return jnp.concatenate([rows, jnp.zeros((ROW_TILE - db, rows.shape[1]), BF16)], axis=0)

    outs = [[] for _ in range(14)]
    for l in range(depth):
        w_l, b_l = w_in[l], b_in[l]
        w_main = jnp.concatenate([w_l[:, :c_f], w_l[:, c_conv:c_i], w_l[:, c_g:]], axis=1)
        b_main = jnp.concatenate([b_l[:c_f], b_l[c_conv:c_i], b_l[c_g:]]).reshape(1, zc_cols)
        gate_pad = LANES - heads - 2 * mh
        w_gate = jnp.concatenate([w_l[:, c_f:c_conv], w_l[:, c_i:c_g],
                                  jnp.zeros((d, gate_pad), F32)], axis=1)
        b_gate = jnp.concatenate([b_l[c_f:c_conv], b_l[c_i:c_g],
                                  jnp.zeros((gate_pad,), F32)]).reshape(1, LANES)

        zc = _inproj(xb_all, w_main, b_main, tn_in)
        zg = _inproj(xb_all, w_gate, b_gate, LANES)

        lf_p, cum, cumt = _prompt_gates(zg, bp, t)
        ya = _prompt_attention(zc, cum, cumt, n_all, bp, t, heads, hd)
        yc, conv_p = _prompt_conv(zc, conv_w, l, zc_conv, n_all, bp, t, cw)
        ym, c_p, n_p_state, m_p = _prompt_mlstm(zc, zg, gain3, l, zc_m, n_all, bp, t, mh, md, i0, f0)

        zs = zc[n_p:n_tok]
        zgs = zg[n_p:n_tok]
        yc_s, conv_s, ym_s, c_s, n_s, m_s, lf_s = _sample_mix(
            zs, zgs, conv_w, gain3, state_conv, state_mlstm_c, state_mlstm_n, state_mlstm_m,
            l, (zc_conv, zc_m), mh, md, i0, f0)
        q_s = zs[:, 0:aw].reshape(db, heads, hd)
        k_s = zs[:, aw:2 * aw].reshape(db, heads, hd)
        v_s = zs[:, 2 * aw:3 * aw].reshape(db, heads, hd)
        lf_new = lf_s[:, 0, 0:heads]
        ya_s = _decode_attention(page_table, q_s, k_s, v_s, lf_new.reshape(db, heads, 1),
                                 cache_k2, cache_v2, cache_lft, l)
        ya = _put_tail_rows(ya, tail(ya_s.reshape(db, aw)))
        yc = _put_tail_rows(yc, tail(yc_s.reshape(db, cw)))
        ym = _put_tail_rows(ym, tail(ym_s.reshape(db, mw)))

        merged = _merge(ya, yc, ym, zc, zc_g, w_branch, l, d)
        y1 = _outproj(merged, x_all, w_out, l, alpha)

        wr = jnp.concatenate([w_router_group[l], w_router_expert[l],
                              jnp.zeros((d, LANES - groups - n_experts), F32)], axis=1)
        br = jnp.concatenate([b_router_group[l], b_router_expert[l],
                              jnp.zeros((LANES - groups - n_experts,), F32)]).reshape(1, LANES)
        x1, x1b, route = _ln_router(y1, ln1_g3, ln1_b3, wr, br, l, groups, epg)

        row_token, pos, tile_expert, n_valid = _expert_rows(route, n_tok, n_experts, moe_tm)
        xs = jnp.take(x1b, row_token, axis=0)
        act = _moe_in(tile_expert, n_valid, xs, w_expert_in, l, de, moe_tm)
        ys = _moe_down(tile_expert, n_valid, act, w_expert_down, l, moe_tm)
        pos_all = jnp.concatenate([pos, jnp.zeros((n_all - n_tok, 2), jnp.int32)], axis=0)
        y0 = jnp.take(ys, pos_all[:, 0], axis=0)
        y1e = jnp.take(ys, pos_all[:, 1], axis=0)
        x_all, xb_all = _combine_ln(x1, y0, y1e, route, ln2_g3, ln2_b3, l, alpha)

        per_layer = (
            zc[:n_p, aw:2 * aw].reshape(bp, t, heads, hd),
            zc[:n_p, 2 * aw:3 * aw].reshape(bp, t, heads, hd),
            lf_p[:, 0:heads].reshape(bp, t, heads),
            k_s.reshape(db, 1, heads, hd),
            v_s.reshape(db, 1, heads, hd),
            lf_new.reshape(db, 1, heads),
            conv_p, conv_s,
            c_p, c_s,
            n_p_state.reshape(bp, mh, md), n_s,
            m_p.reshape(bp, mh), m_s.reshape(db, mh),
        )
        for acc, val in zip(outs, per_layer):
            acc.append(val)

    y_prompt = x_all[:n_p].reshape(bp, t, d)
    y_sample = x_all[n_p:n_tok].reshape(db, 1, d)
    return (y_prompt, y_sample) + tuple(jnp.stack(a, axis=0) for a in outs)
```

```python
import functools
import math

import jax
import jax.numpy as jnp
from jax import lax
from jax.experimental import pallas as pl
from jax.experimental.pallas import tpu as pltpu

F32 = jnp.float32
BF16 = jnp.bfloat16
HIGHEST = lax.Precision.HIGHEST
NEG_INF = float("-inf")

LANES = 128
ROW_TILE = 512
LN_ROW_TILE = 256
MOE_ROW_TILE = 256
V7X_VMEM_BYTES = 64 * 1024 * 1024
VMEM_RESERVE_BYTES = 8 * 1024 * 1024
LN_EPS = 1e-5
NT_DIMS = (((1,), (1,)), ((), ()))


def _vmem_limit(need_bytes):
    return int(min(V7X_VMEM_BYTES - VMEM_RESERVE_BYTES, max(need_bytes, 16 * 1024 * 1024)))


def _params(semantics, need_bytes):
    return pltpu.CompilerParams(dimension_semantics=semantics,
                                vmem_limit_bytes=_vmem_limit(need_bytes))


def _pick(n, prefs):
    for p in prefs:
        if n % p == 0:
            return p
    raise ValueError(f"no tile in {prefs} divides {n}")


def _log_sigmoid(x):
    return jnp.minimum(x, 0.0) - jnp.log1p(jnp.exp(-jnp.abs(x)))


def _sigmoid(x):
    return 1.0 / (1.0 + jnp.exp(-x))


def _cast_rows(src_ref, dst_ref, chunk):
    rows = src_ref.shape[0]

    def body(c, carry):
        r = pl.multiple_of(c * chunk, chunk)
        dst_ref[pl.ds(r, chunk), :] = src_ref[pl.ds(r, chunk), :].astype(BF16)
        return carry

    lax.fori_loop(0, rows // chunk, body, 0)


def _qkv_proj_kernel(*refs, chunk, nq, n_main, n_rows, aliased):
    x_ref, w_ref, b_ref = refs[:3]
    outs = refs[3 + (2 if aliased else 0):]
    qb_ref, kvb_ref, k_ref, v_ref, tail_ref, wb_ref = outs
    j = pl.program_id(0)
    i = pl.program_id(1)

    @pl.when(i == 0)
    def _():
        _cast_rows(w_ref, wb_ref, chunk)

    z = lax.dot_general(x_ref[...], wb_ref[...], NT_DIMS, preferred_element_type=F32) + b_ref[...]

    @pl.when(j < nq)
    def _():
        qb_ref[...] = z.astype(BF16)

    @pl.when(j >= nq)
    def _():
        kvb_ref[...] = z.astype(BF16)

    @pl.when(jnp.logical_and(jnp.logical_and(j >= nq, j < 2 * nq), i < n_main))
    def _():
        k_ref[...] = z

    @pl.when(jnp.logical_and(j >= 2 * nq, i < n_main))
    def _():
        v_ref[...] = z

    @pl.when(jnp.logical_and(j >= nq, i == n_rows - 1))
    def _():
        tail_ref[...] = z


def _qkv_proj(xb, wt_in, b_in3, k_all, v_all, layer, aw, n_p):
    n, k = xb.shape
    depth = wt_in.shape[0]
    tm = ROW_TILE
    tn = _pick(aw, (1024, 512, 256, 128))
    nq = aw // tn
    n_rows = n // tm
    n_main = n_p // tm
    aliased = k_all is not None

    def k_map(j, i):
        row = jnp.where(j < nq, 0, jnp.where(j < 2 * nq, jnp.minimum(i, n_main - 1), n_main - 1))
        return (layer, row, jnp.clip(j - nq, 0, nq - 1))

    def v_map(j, i):
        row = jnp.where(j < 2 * nq, 0, jnp.minimum(i, n_main - 1))
        return (layer, row, jnp.clip(j - 2 * nq, 0, nq - 1))

    def q_map(j, i):
        return (jnp.where(j < nq, i, n_rows - 1), jnp.minimum(j, nq - 1))

    def kvb_map(j, i):
        return (jnp.where(j < nq, 0, i), jnp.maximum(j - nq, 0))

    in_specs = [pl.BlockSpec((tm, k), lambda j, i: (i, 0)),
                pl.BlockSpec((None, tn, k), lambda j, i: (layer, j, 0),
                             pipeline_mode=pl.Buffered(1)),
                pl.BlockSpec((None, 1, tn), lambda j, i: (layer, 0, j))]
    args = [xb, wt_in, b_in3]
    aliases = {}
    if aliased:
        in_specs += [pl.BlockSpec(memory_space=pl.ANY), pl.BlockSpec(memory_space=pl.ANY)]
        args += [k_all, v_all]
        aliases = {3: 2, 4: 3}
    need = k * tn * 4 + k * tn * 2 + 2 * tm * k * 2 + 2 * tm * tn * (2 + 2 + 4 + 4 + 4) + 2 * tm * tn * 4
    return pl.pallas_call(
        functools.partial(_qkv_proj_kernel, chunk=_pick(tn, (64, 16)), nq=nq, n_main=n_main,
                          n_rows=n_rows, aliased=aliased),
        grid=(3 * nq, n_rows),
        in_specs=in_specs,
        out_specs=[pl.BlockSpec((tm, tn), q_map),
                   pl.BlockSpec((tm, tn), kvb_map),
                   pl.BlockSpec((None, tm, tn), k_map),
                   pl.BlockSpec((None, tm, tn), v_map),
                   pl.BlockSpec((tm, tn), lambda j, i: (0, jnp.maximum(j - nq, 0)))],
        out_shape=[jax.ShapeDtypeStruct((n, aw), BF16),
                   jax.ShapeDtypeStruct((n, 2 * aw), BF16),
                   jax.ShapeDtypeStruct((depth, n_p, aw), F32),
                   jax.ShapeDtypeStruct((depth, n_p, aw), F32),
                   jax.ShapeDtypeStruct((tm, 2 * aw), F32)],
        scratch_shapes=[pltpu.VMEM((tn, k), BF16)],
        input_output_aliases=aliases,
        compiler_params=_params(("arbitrary", "arbitrary"), need + (4 << 20)),
        name="qkv_proj",
    )(*args)


def _main_proj_kernel(x_ref, w_ref, wn_ref, b_ref, o_ref, wb_ref,
                      *, chunk, g2_tiles, shift2, shift3):
    j = pl.program_id(0)
    i = pl.program_id(1)
    tn = w_ref.shape[0]

    def prepare(shift):
        def body(c, carry):
            r = pl.multiple_of(c * chunk, chunk)
            wb_ref[pl.ds(r, chunk), :] = w_ref[pl.ds(r + shift, chunk), :].astype(BF16)
            return carry

        lax.fori_loop(0, (tn - shift) // chunk, body, 0)
        wb_ref[tn - shift:tn, :] = wn_ref[0:shift, :].astype(BF16)

    @pl.when(jnp.logical_and(i == 0, j < g2_tiles))
    def _():
        prepare(shift2)

    @pl.when(jnp.logical_and(i == 0, j >= g2_tiles))
    def _():
        prepare(shift3)

    acc = lax.dot_general(x_ref[...], wb_ref[...], NT_DIMS, preferred_element_type=F32)
    o_ref[...] = acc + b_ref[...]


def _main_proj(xb, wt_in, b_main, layer, col0, g2_cols, g3_cols, shift2, shift3):
    n, k = xb.shape
    tm = ROW_TILE
    tn = math.gcd(math.gcd(col0, g2_cols), math.gcd(g3_cols, 1024))
    base = col0 // tn
    nb = -(-shift3 // 8) * 8
    assert shift2 <= shift3 and tn % nb == 0
    chunk = math.gcd(math.gcd(shift2, shift3), 64)
    g2_tiles = g2_cols // tn
    n_tiles = (g2_cols + g3_cols) // tn
    need = k * tn * 4 + 2 * k * nb * 4 + k * tn * 2 + 2 * tm * k * 2 + 3 * tm * tn * 4
    return pl.pallas_call(
        functools.partial(_main_proj_kernel, chunk=chunk, g2_tiles=g2_tiles,
                          shift2=shift2, shift3=shift3),
        grid=(n_tiles, n // tm),
        in_specs=[pl.BlockSpec((tm, k), lambda j, i: (i, 0)),
                  pl.BlockSpec((None, tn, k), lambda j, i: (layer, base + j, 0),
                               pipeline_mode=pl.Buffered(1)),
                  pl.BlockSpec((None, nb, k), lambda j, i: (layer, (base + j + 1) * (tn // nb), 0)),
                  pl.BlockSpec((1, tn), lambda j, i: (0, j))],
        out_specs=pl.BlockSpec((tm, tn), lambda j, i: (i, j)),
        out_shape=jax.ShapeDtypeStruct((n, g2_cols + g3_cols), F32),
        scratch_shapes=[pltpu.VMEM((tn, k), BF16)],
        compiler_params=_params(("arbitrary", "arbitrary"), need + (4 << 20)),
        name="main_proj",
    )(xb, wt_in, wt_in, b_main)


def _gate_proj_kernel(x_ref, wf_ref, wi_ref, b_ref, o_ref, wb_ref):
    nf = wf_ref.shape[0]
    ni = wi_ref.shape[0]

    @pl.when(pl.program_id(0) == 0)
    def _():
        wb_ref[...] = jnp.zeros_like(wb_ref)
        wb_ref[0:nf, :] = wf_ref[...].astype(BF16)
        wb_ref[nf:nf + ni, :] = wi_ref[...].astype(BF16)

    acc = lax.dot_general(x_ref[...], wb_ref[...], NT_DIMS, preferred_element_type=F32)
    o_ref[...] = acc + b_ref[...]


def _gate_proj(xb, wt_in, b_gate, layer, c_f, c_i, heads, mh):
    n, k = xb.shape
    tm = ROW_TILE
    assert c_f % heads == 0 and c_i % (2 * mh) == 0
    need = 4 * (heads + 2 * mh) * k * 4 + LANES * k * 2 + 2 * tm * k * 2 + 3 * tm * LANES * 4
    return pl.pallas_call(
        _gate_proj_kernel,
        grid=(n // tm,),
        in_specs=[pl.BlockSpec((tm, k), lambda i: (i, 0)),
                  pl.BlockSpec((None, heads, k), lambda i: (layer, c_f // heads, 0)),
                  pl.BlockSpec((None, 2 * mh, k), lambda i: (layer, c_i // (2 * mh), 0)),
                  pl.BlockSpec((1, LANES), lambda i: (0, 0))],
        out_specs=pl.BlockSpec((tm, LANES), lambda i: (i, 0)),
        out_shape=jax.ShapeDtypeStruct((n, LANES), F32),
        scratch_shapes=[pltpu.VMEM((LANES, k), BF16)],
        compiler_params=_params(("arbitrary",), need + (4 << 20)),
        name="gate_proj",
    )(xb, wt_in, wt_in, b_gate)


def _merge_kernel(ya_ref, yc_ref, ym_ref, ga_ref, gc_ref, gm_ref, w_ref, o_ref, wb_ref,
                  *, chunk, aw, cw):
    @pl.when(pl.program_id(1) == 0)
    def _():
        _cast_rows(w_ref, wb_ref, chunk)

    da = jnp.dot(ya_ref[...], wb_ref[0:aw, :], preferred_element_type=F32)
    dc = jnp.dot(yc_ref[...], wb_ref[aw:aw + cw, :], preferred_element_type=F32)
    dm = jnp.dot(ym_ref[...], wb_ref[aw + cw:, :], preferred_element_type=F32)
    out = _sigmoid(ga_ref[...]) * da + _sigmoid(gc_ref[...]) * dc + _sigmoid(gm_ref[...]) * dm
    o_ref[...] = out.astype(o_ref.dtype)


def _merge(ya, yc, ym, zc, g_col0, w_branch_l, layer, d):
    n = ya.shape[0]
    aw, cw, mw = ya.shape[1], yc.shape[1], ym.shape[1]
    mix = aw + cw + mw
    tm = ROW_TILE
    tn = _pick(d, (512, 256, 128))
    gb = g_col0 // tn
    nd = d // tn
    need = mix * tn * 4 + mix * tn * 2 + 2 * tm * mix * 2 + 10 * tm * tn * 4
    return pl.pallas_call(
        functools.partial(_merge_kernel, chunk=_pick(mix, (256, 128, 8)), aw=aw, cw=cw),
        grid=(nd, n // tm),
        in_specs=[pl.BlockSpec((tm, aw), lambda j, i: (i, 0)),
                  pl.BlockSpec((tm, cw), lambda j, i: (i, 0)),
                  pl.BlockSpec((tm, mw), lambda j, i: (i, 0)),
                  pl.BlockSpec((tm, tn), lambda j, i: (i, gb + j)),
                  pl.BlockSpec((tm, tn), lambda j, i: (i, gb + nd + j)),
                  pl.BlockSpec((tm, tn), lambda j, i: (i, gb + 2 * nd + j)),
                  pl.BlockSpec((None, mix, tn), lambda j, i: (layer, 0, j),
                               pipeline_mode=pl.Buffered(1))],
        out_specs=pl.BlockSpec((tm, tn), lambda j, i: (i, j)),
        out_shape=jax.ShapeDtypeStruct((n, d), BF16),
        scratch_shapes=[pltpu.VMEM((mix, tn), BF16)],
        compiler_params=_params(("arbitrary", "arbitrary"), need + (4 << 20)),
        name="merge",
    )(ya, yc, ym, zc, zc, zc, w_branch_l)


def _outproj_kernel(m_ref, x_ref, xt_ref, w_ref, o_ref, wb_ref, *, chunk, alpha, n_main):
    i = pl.program_id(1)

    @pl.when(i == 0)
    def _():
        _cast_rows(w_ref, wb_ref, chunk)

    acc = jnp.dot(m_ref[...], wb_ref[...], preferred_element_type=F32)

    @pl.when(i < n_main)
    def _():
        o_ref[...] = alpha * x_ref[...] + acc

    @pl.when(i >= n_main)
    def _():
        o_ref[...] = alpha * xt_ref[...] + acc


def _outproj(mb, x_main, x_tail, w_out, layer, alpha):
    n, k = mb.shape
    d = w_out.shape[2]
    tm = ROW_TILE
    tn = _pick(d, (512, 256, 128))
    n_main = min(x_main.shape[0] // tm, n // tm)
    tail_row = x_tail.shape[0] // tm - 1
    need = 2 * k * tn * 4 + k * tn * 2 + 2 * tm * k * 2 + 8 * tm * tn * 4
    return pl.pallas_call(
        functools.partial(_outproj_kernel, chunk=_pick(k, (256, 128, 8)), alpha=alpha, n_main=n_main),
        grid=(d // tn, n // tm),
        in_specs=[pl.BlockSpec((tm, k), lambda j, i: (i, 0)),
                  pl.BlockSpec((tm, tn), lambda j, i: (jnp.minimum(i, n_main - 1), j)),
                  pl.BlockSpec((tm, tn), lambda j, i: (tail_row, j)),
                  pl.BlockSpec((None, k, tn), lambda j, i: (layer, 0, j))],
        out_specs=pl.BlockSpec((tm, tn), lambda j, i: (i, j)),
        out_shape=jax.ShapeDtypeStruct((n, d), F32),
        scratch_shapes=[pltpu.VMEM((k, tn), BF16)],
        compiler_params=_params(("arbitrary", "arbitrary"), need + (4 << 20)),
        name="outproj",
    )(mb, x_main, x_tail, w_out)


def _gates_kernel(zg_ref, lf_ref, cum_ref, cumt_ref, carry_ref):
    tt = zg_ref.shape[0]

    @pl.when(pl.program_id(1) == 0)
    def _():
        carry_ref[...] = jnp.zeros_like(carry_ref)

    lf = _log_sigmoid(zg_ref[...])
    row = lax.broadcasted_iota(jnp.int32, (tt, tt), 0)
    col = lax.broadcasted_iota(jnp.int32, (tt, tt), 1)
    tri = (row >= col).astype(F32)
    cum = jnp.dot(tri, lf, precision=HIGHEST, preferred_element_type=F32) + carry_ref[...]
    carry_ref[...] = cum[tt - 1:tt, :]
    lf_ref[...] = lf
    cum_ref[...] = cum
    cumt_ref[0] = cum.T


def _prompt_gates(zg, b, t):
    tt = _pick(t, (256, 128))
    nt = t // tt
    return pl.pallas_call(
        _gates_kernel,
        grid=(b, nt),
        in_specs=[pl.BlockSpec((tt, LANES), lambda bi, ti: (bi * nt + ti, 0))],
        out_specs=[pl.BlockSpec((tt, LANES), lambda bi, ti: (bi * nt + ti, 0)),
                   pl.BlockSpec((tt, LANES), lambda bi, ti: (bi * nt + ti, 0)),
                   pl.BlockSpec((1, LANES, tt), lambda bi, ti: (bi, 0, ti))],
        out_shape=[jax.ShapeDtypeStruct((b * t, LANES), F32),
                   jax.ShapeDtypeStruct((b * t, LANES), F32),
                   jax.ShapeDtypeStruct((b, LANES, t), F32)],
        scratch_shapes=[pltpu.VMEM((1, LANES), F32)],
        compiler_params=_params(("arbitrary", "arbitrary"), 0),
        name="prompt_gates",
    )(zg)


def _attn_kernel(q_ref, k_ref, v_ref, cq_ref, ck_ref, o_ref, m_ref, l_ref, acc_ref,
                 *, heads, hd, scale, qs):
    qi = pl.program_id(1)
    ki = pl.program_id(2)
    tq = q_ref.shape[0]
    tk = k_ref.shape[0]

    @pl.when(ki == 0)
    def _():
        m_ref[...] = jnp.full(m_ref.shape, -1e30, F32)
        l_ref[...] = jnp.zeros_like(l_ref)
        acc_ref[...] = jnp.zeros_like(acc_ref)

    def block(diagonal):
        cq = cq_ref[...]
        ck = ck_ref[0]
        for h in range(heads):
            sl = slice(h * hd, (h + 1) * hd)
            kh = k_ref[:, sl]
            vh = v_ref[:, sl]
            ckh = ck[h:h + 1, :]
            for r in range(tq // qs):
                rs = slice(r * qs, (r + 1) * qs)
                s = lax.dot_general(q_ref[rs, sl], kh, NT_DIMS, preferred_element_type=F32) * scale
                s = s + (cq[rs, h:h + 1] - ckh)
                if diagonal:
                    row = lax.broadcasted_iota(jnp.int32, (qs, tk), 0) + r * qs
                    col = lax.broadcasted_iota(jnp.int32, (qs, tk), 1)
                    s = jnp.where(col <= row, s, -1e30)
                m_old = m_ref[h, rs]
                m_new = jnp.maximum(m_old, jnp.max(s, axis=1, keepdims=True))
                p = jnp.exp(s - m_new)
                a = jnp.exp(m_old - m_new)
                l_ref[h, rs] = a * l_ref[h, rs] + jnp.sum(p, axis=1, keepdims=True)
                acc_ref[rs, sl] = a * acc_ref[rs, sl] + jnp.dot(p.astype(BF16), vh,
                                                                preferred_element_type=F32)
                m_ref[h, rs] = m_new

    @pl.when(ki < qi)
    def _():
        block(False)

    @pl.when(ki == qi)
    def _():
        block(True)
        for h in range(heads):
            sl = slice(h * hd, (h + 1) * hd)
            o_ref[:, sl] = (acc_ref[:, sl] / l_ref[h]).astype(o_ref.dtype)


def _prompt_attention(qb, kvb, cum, cumt, n_all, b, t, heads, hd):
    aw = heads * hd
    tq = _pick(t, (256, 128))
    nt = t // tq
    need = 2 * 3 * tq * aw * 2 + 2 * tq * aw * 2 + tq * aw * 4 + 2 * heads * tq * LANES * 4
    return pl.pallas_call(
        functools.partial(_attn_kernel, heads=heads, hd=hd, scale=hd ** -0.5, qs=tq),
        grid=(b, nt, nt),
        in_specs=[pl.BlockSpec((tq, aw), lambda bi, qi, ki: (bi * nt + qi, 0)),
                  pl.BlockSpec((tq, aw), lambda bi, qi, ki: (bi * nt + jnp.minimum(ki, qi), 0)),
                  pl.BlockSpec((tq, aw), lambda bi, qi, ki: (bi * nt + jnp.minimum(ki, qi), 1)),
                  pl.BlockSpec((tq, LANES), lambda bi, qi, ki: (bi * nt + qi, 0)),
                  pl.BlockSpec((1, LANES, tq), lambda bi, qi, ki: (bi, 0, jnp.minimum(ki, qi)))],
        out_specs=pl.BlockSpec((tq, aw), lambda bi, qi, ki: (bi * nt + qi, 0)),
        out_shape=jax.ShapeDtypeStruct((n_all, aw), BF16),
        scratch_shapes=[pltpu.VMEM((heads, tq, 1), F32),
                        pltpu.VMEM((heads, tq, 1), F32),
                        pltpu.VMEM((tq, aw), F32)],
        compiler_params=_params(("arbitrary", "arbitrary", "arbitrary"), need + (8 << 20)),
        name="prompt_attention",
    )(qb, kvb, kvb, cum, cumt)


def _decode_kernel(pt_ref, q_ref, kn_ref, vn_ref, lfn_ref, *rest, pps, heads, scale):
    del pt_ref
    k_refs = rest[:pps]
    v_refs = rest[pps:2 * pps]
    lf_refs = rest[2 * pps:3 * pps]
    o_ref = rest[3 * pps]
    m_ref, l_ref, acc_ref, carry_ref = rest[3 * pps + 1:]
    p_idx = pl.program_id(1)
    ps = lf_refs[0].shape[1]
    hd = q_ref.shape[2]
    q = q_ref[0]

    @pl.when(p_idx == 0)
    def _():
        m_ref[...] = jnp.sum(q * kn_ref[0], axis=1, keepdims=True) * scale
        l_ref[...] = jnp.ones_like(l_ref)
        acc_ref[...] = vn_ref[0]
        carry_ref[...] = lfn_ref[0]

    q8 = [jnp.broadcast_to(q[h:h + 1, :], (8, hd)).astype(BF16) for h in range(heads)]
    row = lax.broadcasted_iota(jnp.int32, (ps, ps), 0)
    col = lax.broadcasted_iota(jnp.int32, (ps, ps), 1)
    later = (row > col).astype(F32)
    for i in range(pps):
        lf = lf_refs[i][...]
        bias = jnp.dot(lf, later, precision=HIGHEST, preferred_element_type=F32) + carry_ref[...]
        carry_ref[...] = carry_ref[...] + jnp.sum(lf, axis=1, keepdims=True)
        rows = []
        for h in range(heads):
            kh = k_refs[i][pl.ds(h, ps, stride=heads), :].astype(BF16)
            rows.append(lax.dot_general(q8[h], kh, NT_DIMS, preferred_element_type=F32)[0:1, :])
        s = jnp.concatenate(rows, axis=0) * scale + bias
        m_old = m_ref[...]
        m_new = jnp.maximum(m_old, jnp.max(s, axis=1, keepdims=True))
        pr = jnp.exp(s - m_new)
        a = jnp.exp(m_old - m_new)
        l_ref[...] = a * l_ref[...] + jnp.sum(pr, axis=1, keepdims=True)
        outs = []
        for h in range(heads):
            ph = jnp.broadcast_to(pr[h:h + 1, :], (8, ps)).astype(BF16)
            vh = v_refs[i][pl.ds(h, ps, stride=heads), :].astype(BF16)
            outs.append(jnp.dot(ph, vh, preferred_element_type=F32)[0:1, :])
        acc_ref[...] = a * acc_ref[...] + jnp.concatenate(outs, axis=0)
        m_ref[...] = m_new

    @pl.when(p_idx == pl.num_programs(1) - 1)
    def _():
        o_ref[0] = acc_ref[...] / l_ref[...]


def _decode_attention(page_table, q, k_new, v_new, lf_new, cache_k2, cache_v2, cache_lft, layer):
    db, heads, hd = q.shape
    n_pages = page_table.shape[1]
    ps = cache_lft.shape[3]
    pps = _pick(n_pages, (4, 2, 1))
    steps = n_pages // pps
    pt_flat = page_table.reshape(-1)

    def page_map(i):
        def index_map(b, p, pt):
            return (layer, pt[b * n_pages + (n_pages - 1 - (p * pps + i))], 0, 0)
        return index_map

    small = lambda b, p, pt: (b, 0, 0)
    in_specs = [pl.BlockSpec((1, heads, hd), small), pl.BlockSpec((1, heads, hd), small),
                pl.BlockSpec((1, heads, hd), small), pl.BlockSpec((1, heads, 1), small)]
    in_specs += [pl.BlockSpec((None, None, ps * heads, hd), page_map(i)) for i in range(pps)]
    in_specs += [pl.BlockSpec((None, None, ps * heads, hd), page_map(i)) for i in range(pps)]
    in_specs += [pl.BlockSpec((None, None, heads, ps), page_map(i)) for i in range(pps)]
    need = 2 * 2 * pps * ps * heads * hd * 4
    grid_spec = pltpu.PrefetchScalarGridSpec(
        num_scalar_prefetch=1,
        grid=(db, steps),
        in_specs=in_specs,
        out_specs=pl.BlockSpec((1, heads, hd), small),
        scratch_shapes=[pltpu.VMEM((heads, 1), F32), pltpu.VMEM((heads, 1), F32),
                        pltpu.VMEM((heads, hd), F32), pltpu.VMEM((heads, 1), F32)],
    )
    return pl.pallas_call(
        functools.partial(_decode_kernel, pps=pps, heads=heads, scale=hd ** -0.5),
        grid_spec=grid_spec,
        out_shape=jax.ShapeDtypeStruct((db, heads, hd), F32),
        compiler_params=_params(("arbitrary", "arbitrary"), need + (8 << 20)),
        name="decode_attention",
    )(pt_flat, q, k_new, v_new, lf_new,
      *([cache_k2] * pps), *([cache_v2] * pps), *([cache_lft] * pps))


def _conv_kernel(cb_ref, cc_ref, ch_ref, w_ref, y_ref, cn_ref):
    t = cc_ref.shape[0]
    u = cc_ref[...] * ch_ref[...]
    row = lax.broadcasted_iota(jnp.int32, u.shape, 0)
    u1 = jnp.where(row >= 1, pltpu.roll(u, 1, axis=0), 0.0)
    u2 = jnp.where(row >= 2, pltpu.roll(u, 2, axis=0), 0.0)
    w = w_ref[...]
    y = cb_ref[...] * (w[0:1, :] * u2 + w[1:2, :] * u1 + w[2:3, :] * u)
    y_ref[...] = y.astype(y_ref.dtype)
    cn_ref[0] = u[t - 2:t, :]


def _prompt_conv(zc, conv_w, layer, col0, n_all, b, t, cw):
    tc = _pick(cw, (256, 128))
    cb0 = col0 // tc
    nc = cw // tc
    kk = conv_w.shape[1]
    need = 2 * 3 * t * tc * 4 + 2 * t * tc * 2 + 6 * t * tc * 4
    return pl.pallas_call(
        _conv_kernel,
        grid=(b, nc),
        in_specs=[pl.BlockSpec((t, tc), lambda bi, j: (bi, cb0 + j)),
                  pl.BlockSpec((t, tc), lambda bi, j: (bi, cb0 + nc + j)),
                  pl.BlockSpec((t, tc), lambda bi, j: (bi, cb0 + 2 * nc + j)),
                  pl.BlockSpec((None, kk, tc), lambda bi, j: (layer, 0, j))],
        out_specs=[pl.BlockSpec((t, tc), lambda bi, j: (bi, j)),
                   pl.BlockSpec((1, kk - 1, tc), lambda bi, j: (bi, 0, j))],
        out_shape=[jax.ShapeDtypeStruct((n_all, cw), BF16),
                   jax.ShapeDtypeStruct((b, kk - 1, cw), F32)],
        compiler_params=_params(("arbitrary", "arbitrary"), need + (4 << 20)),
        name="prompt_conv",
    )(zc, zc, zc, conv_w)


def _head_norm(x, gain_row):
    mu = jnp.mean(x, axis=1, keepdims=True)
    xc = x - mu
    var = jnp.mean(xc * xc, axis=1, keepdims=True)
    return xc * lax.rsqrt(var + LN_EPS) * gain_row


def _mlstm_kernel(q_ref, k_ref, v_ref, o_ref, zg_ref, gain_ref, y_ref, c_out, n_out, m_out,
                  c_ref, n_ref, m_ref, *, heads, md, i0, f0):
    ci = pl.program_id(1)
    lc = q_ref.shape[0]

    @pl.when(ci == 0)
    def _():
        c_ref[...] = jnp.zeros_like(c_ref)
        n_ref[...] = jnp.zeros_like(n_ref)
        m_ref[...] = jnp.zeros_like(m_ref)

    zg = zg_ref[...]
    zgt = zg.T
    row = lax.broadcasted_iota(jnp.int32, (lc, lc), 0)
    col = lax.broadcasted_iota(jnp.int32, (lc, lc), 1)
    causal = col <= row
    bcols = jnp.dot(causal.astype(F32), _log_sigmoid(zg), precision=HIGHEST,
                    preferred_element_type=F32)
    brows = bcols.T
    kscale = md ** -0.5
    for h in range(heads):
        sl = slice(h * md, (h + 1) * md)
        bcol = bcols[:, f0 + h:f0 + h + 1]
        brow = brows[f0 + h:f0 + h + 1, :]
        icol = zg[:, i0 + h:i0 + h + 1]
        irow = zgt[i0 + h:i0 + h + 1, :]
        m_prev = m_ref[h]
        cst = c_ref[h]
        nst = n_ref[h]
        qh = q_ref[:, sl]
        kh = k_ref[:, sl] * kscale
        qb = qh.astype(BF16)
        kb = kh.astype(BF16)
        vb = v_ref[:, sl].astype(BF16)

        dmat = jnp.where(causal, bcol - brow + irow, NEG_INF)
        inter = bcol + m_prev
        m_t = jnp.maximum(jnp.max(dmat, axis=1, keepdims=True), inter)
        w_intra = jnp.exp(dmat - m_t)
        w_inter = jnp.exp(inter - m_t)
        s = lax.dot_general(qb, kb, NT_DIMS, preferred_element_type=F32) * w_intra
        num = jnp.dot(s.astype(BF16), vb, preferred_element_type=F32) + w_inter * lax.dot_general(
            qb, cst.astype(BF16), NT_DIMS, preferred_element_type=F32)
        den = jnp.sum(s, axis=1, keepdims=True) + w_inter * jnp.sum(qh * nst, axis=1, keepdims=True)
        hh = num / jnp.maximum(jnp.abs(den), jnp.exp(-m_t))
        og = _sigmoid(o_ref[:, sl])
        y_ref[:, sl] = _head_norm(og * hh, gain_ref[:, sl]).astype(y_ref.dtype)

        g = bcol[lc - 1:lc, :]
        acol = g - bcol + icol
        m_new = jnp.maximum(g + m_prev, jnp.max(acol, axis=0, keepdims=True))
        decay = jnp.exp(g + m_prev - m_new)
        wa = jnp.exp(acol - m_new)
        vwt = (v_ref[:, sl] * wa).T.astype(BF16)
        c_ref[h] = decay * cst + jnp.dot(vwt, kb, preferred_element_type=F32)
        n_ref[h] = decay * nst + jnp.sum(kh * wa, axis=0, keepdims=True)
        m_ref[h] = m_new

    @pl.when(ci == pl.num_programs(1) - 1)
    def _():
        c_out[0] = c_ref[...]
        n_out[0] = n_ref[...]
        m_out[0] = m_ref[...]


def _prompt_mlstm(zc, zg, gain, layer, col0, n_all, b, t, heads, md, i0, f0):
    mw = heads * md
    lc = _pick(t, (256, 128))
    nch = t // lc
    assert col0 % mw == 0
    qb0 = col0 // mw
    need = 2 * 4 * lc * mw * 4 + 2 * lc * mw * 2 + 3 * heads * md * md * 4 + 12 * lc * lc * 4
    zblock = lambda off: pl.BlockSpec((lc, mw), lambda bi, ci: (bi * nch + ci, qb0 + off))
    return pl.pallas_call(
        functools.partial(_mlstm_kernel, heads=heads, md=md, i0=i0, f0=f0),
        grid=(b, nch),
        in_specs=[zblock(0), zblock(1), zblock(2), zblock(3),
                  pl.BlockSpec((lc, LANES), lambda bi, ci: (bi * nch + ci, 0)),
                  pl.BlockSpec((None, 1, mw), lambda bi, ci: (layer, 0, 0))],
        out_specs=[pl.BlockSpec((lc, mw), lambda bi, ci: (bi * nch + ci, 0)),
                   pl.BlockSpec((1, heads, md, md), lambda bi, ci: (bi, 0, 0, 0)),
                   pl.BlockSpec((1, heads, 1, md), lambda bi, ci: (bi, 0, 0, 0)),
                   pl.BlockSpec((1, heads, 1, 1), lambda bi, ci: (bi, 0, 0, 0))],
        out_shape=[jax.ShapeDtypeStruct((n_all, mw), BF16),
                   jax.ShapeDtypeStruct((b, heads, md, md), F32),
                   jax.ShapeDtypeStruct((b, heads, 1, md), F32),
                   jax.ShapeDtypeStruct((b, heads, 1, 1), F32)],
        scratch_shapes=[pltpu.VMEM((heads, md, md), F32),
                        pltpu.VMEM((heads, 1, md), F32),
                        pltpu.VMEM((heads, 1, 1), F32)],
        compiler_params=_params(("arbitrary", "arbitrary"), need + (8 << 20)),
        name="prompt_mlstm",
    )(zc, zc, zc, zc, zg, gain)


def _sample_mix_kernel(cb_ref, cc_ref, ch_ref, cw_ref, prev_ref, q_ref, k_ref, v_ref, vcol_ref,
                       o_ref, zg_ref, gain_ref, c0_ref, n0_ref, m0_ref,
                       yc_ref, cn_ref, ym_ref, c1_ref, n1_ref, m1_ref, lf_ref,
                       *, heads, md, i0, f0):
    u = cc_ref[0] * ch_ref[0]
    prev = prev_ref[0]
    w = cw_ref[...]
    yc_ref[0] = cb_ref[0] * (w[0:1, :] * prev[0:1, :] + w[1:2, :] * prev[1:2, :] + w[2:3, :] * u)
    cn_ref[0, 0:1, :] = prev[1:2, :]
    cn_ref[0, 1:2, :] = u

    zg = zg_ref[0]
    lfz = _log_sigmoid(zg)
    lf_ref[0] = lfz
    lane = lax.broadcasted_iota(jnp.int32, (1, heads), 1)
    m_row = jnp.zeros((1, heads), F32)
    kscale = md ** -0.5
    for h in range(heads):
        sl = slice(h * md, (h + 1) * md)
        qh = q_ref[0][:, sl]
        kh = k_ref[0][:, sl] * kscale
        vh = v_ref[0][:, sl]
        ig = zg[:, i0 + h:i0 + h + 1]
        fg = lfz[:, f0 + h:f0 + h + 1]
        m_prev = m0_ref[0][:, h:h + 1]
        cst = c0_ref[0, h]
        nst = n0_ref[0][h:h + 1, :]
        inter = fg + m_prev
        m_t = jnp.maximum(ig, inter)
        w_intra = jnp.exp(ig - m_t)
        w_inter = jnp.exp(inter - m_t)
        s = jnp.sum(qh * kh, axis=1, keepdims=True) * w_intra
        q8 = jnp.broadcast_to(qh, (8, md))
        cq = lax.dot_general(q8, cst, NT_DIMS, precision=HIGHEST,
                             preferred_element_type=F32)[0:1, :]
        num = s * vh + w_inter * cq
        den = s + w_inter * jnp.sum(nst * qh, axis=1, keepdims=True)
        hh = num / jnp.maximum(jnp.abs(den), jnp.exp(-m_t))
        og = _sigmoid(o_ref[0][:, sl])
        ym_ref[0, :, sl] = _head_norm(og * hh, gain_ref[:, sl])
        m_new = jnp.maximum(inter, ig)
        decay = jnp.exp(inter - m_new)
        wa = jnp.exp(ig - m_new)
        c1_ref[0, h] = decay * cst + (vcol_ref[0, h] * wa) * kh
        n1_ref[0, h:h + 1, :] = decay * nst + wa * kh
        m_row = jnp.where(lane == h, m_new, m_row)
    m1_ref[0] = m_row


def _sample_mix(zs, zgs, conv_w, gain, state_conv, c0, n0, m0, layer, cols, heads, md, i0, f0):
    db = zs.shape[0]
    cw = state_conv.shape[-1]
    mw = heads * md
    kk = conv_w.shape[1]
    cb0, q0 = cols
    seg = lambda a, width: zs[:, a:a + width].reshape(db, 1, width)
    cb, cc, ch = seg(cb0, cw), seg(cb0 + cw, cw), seg(cb0 + 2 * cw, cw)
    q, k, v, o = (seg(q0 + i * mw, mw) for i in range(4))
    vcol = v.reshape(db, heads, md, 1)
    zg3 = zgs.reshape(db, 1, LANES)
    m03 = m0.reshape(m0.shape[0], db, 1, heads)
    row3 = lambda width: pl.BlockSpec((1, 1, width), lambda b: (b, 0, 0))
    in_specs = [row3(cw), row3(cw), row3(cw),
                pl.BlockSpec((None, kk, cw), lambda b: (layer, 0, 0)),
                pl.BlockSpec((None, 1, kk - 1, cw), lambda b: (layer, b, 0, 0)),
                row3(mw), row3(mw), row3(mw),
                pl.BlockSpec((1, heads, md, 1), lambda b: (b, 0, 0, 0)),
                row3(mw), row3(LANES),
                pl.BlockSpec((None, 1, mw), lambda b: (layer, 0, 0)),
                pl.BlockSpec((None, 1, heads, md, md), lambda b: (layer, b, 0, 0, 0)),
                pl.BlockSpec((None, 1, heads, md), lambda b: (layer, b, 0, 0)),
                pl.BlockSpec((None, 1, 1, heads), lambda b: (layer, b, 0, 0))]
    out_specs = [row3(cw),
                 pl.BlockSpec((1, kk - 1, cw), lambda b: (b, 0, 0)),
                 row3(mw),
                 pl.BlockSpec((1, heads, md, md), lambda b: (b, 0, 0, 0)),
                 pl.BlockSpec((1, heads, md), lambda b: (b, 0, 0)),
                 pl.BlockSpec((1, 1, heads), lambda b: (b, 0, 0)),
                 row3(LANES)]
    out_shape = [jax.ShapeDtypeStruct((db, 1, cw), F32),
                 jax.ShapeDtypeStruct((db, kk - 1, cw), F32),
                 jax.ShapeDtypeStruct((db, 1, mw), F32),
                 jax.ShapeDtypeStruct((db, heads, md, md), F32),
                 jax.ShapeDtypeStruct((db, heads, md), F32),
                 jax.ShapeDtypeStruct((db, 1, heads), F32),
                 jax.ShapeDtypeStruct((db, 1, LANES), F32)]
    need = 4 * heads * md * md * 4 + 2 * heads * md * LANES * 4
    return pl.pallas_call(
        functools.partial(_sample_mix_kernel, heads=heads, md=md, i0=i0, f0=f0),
        grid=(db,),
        in_specs=in_specs,
        out_specs=out_specs,
        out_shape=out_shape,
        compiler_params=_params(("arbitrary",), need + (8 << 20)),
        name="sample_mix",
    )(cb, cc, ch, conv_w, state_conv, q, k, v, vcol, o, zg3, gain, c0, n0, m03)


def _put_rows_kernel(dst_ref, rows_ref, o_ref):
    del dst_ref
    o_ref[...] = rows_ref[...]


def _put_tail_rows(dst, rows):
    n, w = dst.shape
    last = n // ROW_TILE - 1
    return pl.pallas_call(
        _put_rows_kernel,
        grid=(1,),
        in_specs=[pl.BlockSpec(memory_space=pl.ANY),
                  pl.BlockSpec((ROW_TILE, w), lambda i: (0, 0))],
        out_specs=pl.BlockSpec((ROW_TILE, w), lambda i: (last, 0)),
        out_shape=jax.ShapeDtypeStruct(dst.shape, dst.dtype),
        input_output_aliases={0: 0},
        name="put_tail_rows",
    )(dst, rows)


def _layer_norm_rows(y, g, b):
    mu = jnp.mean(y, axis=1, keepdims=True)
    yc = y - mu
    var = jnp.mean(yc * yc, axis=1, keepdims=True)
    return yc * lax.rsqrt(var + LN_EPS) * g + b


def _first_argmax(x, width):
    lane = lax.broadcasted_iota(jnp.int32, x.shape, 1).astype(F32)
    mx = jnp.max(x, axis=1, keepdims=True)
    idx = jnp.min(jnp.where(x == mx, lane, float(width)), axis=1, keepdims=True)
    return mx, idx


def _ln_router_kernel(y_ref, g_ref, b_ref, wr_ref, br_ref, x_ref, r_ref, *, groups, epg):
    x1 = _layer_norm_rows(y_ref[...], g_ref[...], b_ref[...])
    x_ref[...] = x1
    logits = jnp.dot(x1, wr_ref[...], precision=HIGHEST, preferred_element_type=F32) + br_ref[...]
    gl = logits[:, 0:groups]
    gmax, gstar = _first_argmax(gl, groups)
    p_star = 1.0 / jnp.sum(jnp.exp(gl - gmax), axis=1, keepdims=True)
    elg = logits[:, groups:groups + epg]
    for g in range(1, groups):
        elg = jnp.where(gstar == g, logits[:, groups + g * epg:groups + (g + 1) * epg], elg)
    v0, i0 = _first_argmax(elg, epg)
    lane = lax.broadcasted_iota(jnp.int32, elg.shape, 1).astype(F32)
    v1, i1 = _first_argmax(jnp.where(lane == i0, NEG_INF, elg), epg)
    e1 = jnp.exp(v1 - v0)
    w0 = p_star / (1.0 + e1)
    w1 = p_star * e1 / (1.0 + e1)
    out_lane = lax.broadcasted_iota(jnp.int32, r_ref.shape, 1)
    ex0 = (gstar * epg + i0).astype(F32)
    ex1 = (gstar * epg + i1).astype(F32)
    r = jnp.where(out_lane == 0, ex0, 0.0)
    r = jnp.where(out_lane == 1, ex1, r)
    r = jnp.where(out_lane == 2, w0, r)
    r = jnp.where(out_lane == 3, w1, r)
    r_ref[...] = r


def _ln_router(y, g, b, wr, br, layer, groups, epg):
    n, d = y.shape
    tm = LN_ROW_TILE
    need = 2 * tm * d * 4 * 2 + 2 * d * LANES * 4 + 6 * tm * d * 4
    vec = pl.BlockSpec((None, 1, d), lambda i: (layer, 0, 0))
    return pl.pallas_call(
        functools.partial(_ln_router_kernel, groups=groups, epg=epg),
        grid=(n // tm,),
        in_specs=[pl.BlockSpec((tm, d), lambda i: (i, 0)), vec, vec,
                  pl.BlockSpec((d, LANES), lambda i: (0, 0)),
                  pl.BlockSpec((1, LANES), lambda i: (0, 0))],
        out_specs=[pl.BlockSpec((tm, d), lambda i: (i, 0)),
                   pl.BlockSpec((tm, LANES), lambda i: (i, 0))],
        out_shape=[jax.ShapeDtypeStruct((n, d), F32),
                   jax.ShapeDtypeStruct((n, LANES), F32)],
        compiler_params=_params(("arbitrary",), need + (4 << 20)),
        name="ln_router",
    )(y, g, b, wr, br)


def _moe_gather_kernel(tok_ref, nv_ref, x_hbm, o_ref, buf_ref, sem_ref, *, tm):
    t = pl.program_id(0)
    nv = nv_ref[0]
    slot = t % 2

    def row_copy(tok, k, s):
        return pltpu.make_async_copy(x_hbm.at[pl.ds(tok, 1), :], buf_ref.at[s, pl.ds(k, 1), :],
                                     sem_ref.at[s])

    def issue(tile, s):
        def body(k, carry):
            row_copy(tok_ref[tile * tm + k], k, s).start()
            return carry

        lax.fori_loop(0, tm, body, 0, unroll=8)

    @pl.when(t == 0)
    def _():
        issue(0, 0)

    @pl.when(t + 1 < nv)
    def _():
        issue(t + 1, 1 - slot)

    @pl.when(t < nv)
    def _():
        def body(k, carry):
            row_copy(0, k, slot).wait()
            return carry

        lax.fori_loop(0, tm, body, 0, unroll=8)
        o_ref[...] = buf_ref[slot].astype(BF16)


def _moe_gather(row_token, n_valid, x1, tm):
    d = x1.shape[1]
    r = row_token.shape[0]
    grid_spec = pltpu.PrefetchScalarGridSpec(
        num_scalar_prefetch=2,
        grid=(r // tm,),
        in_specs=[pl.BlockSpec(memory_space=pl.ANY)],
        out_specs=pl.BlockSpec((tm, d), lambda t, tok, nv: (jnp.minimum(t, nv[0] - 1), 0)),
        scratch_shapes=[pltpu.VMEM((2, tm, d), F32), pltpu.SemaphoreType.DMA((2,))],
    )
    need = 2 * tm * d * 4 + 2 * tm * d * 2 + 2 * tm * d * 4
    return pl.pallas_call(
        functools.partial(_moe_gather_kernel, tm=tm),
        grid_spec=grid_spec,
        out_shape=jax.ShapeDtypeStruct((r, d), BF16),
        compiler_params=_params(("arbitrary",), need + (4 << 20)),
        name="moe_gather",
    )(row_token, n_valid, x1)


def _moe_in_kernel(te_ref, nv_ref, x_ref, wg_ref, wv_ref, a_ref, wgb_ref, wvb_ref, *, chunk):
    t = pl.program_id(1)
    prev = te_ref[jnp.maximum(t - 1, 0)]
    fresh = jnp.logical_or(t == 0, te_ref[t] != prev)

    @pl.when(jnp.logical_and(fresh, t < nv_ref[0]))
    def _():
        _cast_rows(wg_ref, wgb_ref, chunk)
        _cast_rows(wv_ref, wvb_ref, chunk)

    @pl.when(t < nv_ref[0])
    def _():
        x = x_ref[...]
        hg = jnp.dot(x, wgb_ref[...], preferred_element_type=F32)
        hv = jnp.dot(x, wvb_ref[...], preferred_element_type=F32)
        a_ref[...] = (hg * _sigmoid(hg) * hv).astype(a_ref.dtype)


def _moe_in(tile_expert, n_valid, xs, w_e_in, layer, de, tm):
    r, d = xs.shape
    tf = _pick(de, (512, 256, 128))
    nf = de // tf
    n_tiles = r // tm
    need = 2 * 2 * d * tf * 4 + 2 * d * tf * 2 + 2 * tm * d * 2 + 6 * tm * tf * 4
    row = lambda j, t, te, nv: (jnp.minimum(t, nv[0] - 1), 0)
    grid_spec = pltpu.PrefetchScalarGridSpec(
        num_scalar_prefetch=2,
        grid=(nf, n_tiles),
        in_specs=[pl.BlockSpec((tm, d), row),
                  pl.BlockSpec((None, None, d, tf), lambda j, t, te, nv: (layer, te[t], 0, j)),
                  pl.BlockSpec((None, None, d, tf), lambda j, t, te, nv: (layer, te[t], 0, nf + j))],
        out_specs=pl.BlockSpec((tm, tf), lambda j, t, te, nv: (jnp.minimum(t, nv[0] - 1), j)),
        scratch_shapes=[pltpu.VMEM((d, tf), BF16), pltpu.VMEM((d, tf), BF16)],
    )
    return pl.pallas_call(
        functools.partial(_moe_in_kernel, chunk=_pick(d, (256, 128, 8))),
        grid_spec=grid_spec,
        out_shape=jax.ShapeDtypeStruct((r, de), BF16),
        compiler_params=_params(("arbitrary", "arbitrary"), need + (4 << 20)),
        name="moe_in",
    )(tile_expert, n_valid, xs, w_e_in, w_e_in)


def _moe_down_kernel(te_ref, nv_ref, dest_ref, a_ref, w_ref, out_hbm, wb_ref, y_ref, sem_ref,
                     *, chunk, tm):
    t = pl.program_id(0)
    last = pl.num_programs(0) - 1
    nv = nv_ref[0]
    slot = t % 2
    prev = te_ref[jnp.maximum(t - 1, 0)]
    fresh = jnp.logical_or(t == 0, te_ref[t] != prev)

    def row_copy(k, s, dest):
        return pltpu.make_async_copy(y_ref.at[s, pl.ds(k, 1), :], out_hbm.at[pl.ds(dest, 1), :],
                                     sem_ref.at[s])

    def drain(s):
        def body(k, carry):
            row_copy(k, s, 0).wait()
            return carry

        lax.fori_loop(0, tm, body, 0, unroll=8)

    @pl.when(jnp.logical_and(t >= 2, t - 2 < nv))
    def _():
        drain(slot)

    @pl.when(jnp.logical_and(fresh, t < nv))
    def _():
        _cast_rows(w_ref, wb_ref, chunk)

    @pl.when(t < nv)
    def _():
        y_ref[slot] = jnp.dot(a_ref[...], wb_ref[...], preferred_element_type=F32)

        def body(k, carry):
            row_copy(k, slot, dest_ref[t * tm + k]).start()
            return carry

        lax.fori_loop(0, tm, body, 0, unroll=8)

    @pl.when(jnp.logical_and(t == last, jnp.logical_and(t >= 1, t - 1 < nv)))
    def _():
        drain(1 - slot)

    @pl.when(jnp.logical_and(t == last, t < nv))
    def _():
        drain(slot)


def _moe_down(tile_expert, n_valid, row_dest, a, w_e_down, layer, n_out_rows, tm):
    r, de = a.shape
    d = w_e_down.shape[3]
    n_tiles = r // tm
    need = 2 * de * d * 4 + de * d * 2 + 2 * tm * de * 2 + 3 * tm * d * 4
    grid_spec = pltpu.PrefetchScalarGridSpec(
        num_scalar_prefetch=3,
        grid=(n_tiles,),
        in_specs=[pl.BlockSpec((tm, de), lambda t, te, nv, dest: (jnp.minimum(t, nv[0] - 1), 0)),
                  pl.BlockSpec((None, None, de, d), lambda t, te, nv, dest: (layer, te[t], 0, 0))],
        out_specs=pl.BlockSpec(memory_space=pl.ANY),
        scratch_shapes=[pltpu.VMEM((de, d), BF16), pltpu.VMEM((2, tm, d), F32),
                        pltpu.SemaphoreType.DMA((2,))],
    )
    return pl.pallas_call(
        functools.partial(_moe_down_kernel, chunk=_pick(de, (256, 128, 8)), tm=tm),
        grid_spec=grid_spec,
        out_shape=jax.ShapeDtypeStruct((n_out_rows, d), F32),
        compiler_params=_params(("arbitrary",), need + (4 << 20)),
        name="moe_down",
    )(tile_expert, n_valid, row_dest, a, w_e_down)


def _combine_ln_kernel(x_ref, y0_ref, y1_ref, r_ref, g_ref, b_ref, *outs, alpha, n_main, final):
    r = r_ref[...]
    y = alpha * x_ref[...] + r[:, 2:3] * y0_ref[...] + r[:, 3:4] * y1_ref[...]
    x2 = _layer_norm_rows(y, g_ref[...], b_ref[...])
    if final:
        main_ref, tail_ref = outs
        i = pl.program_id(0)

        @pl.when(i < n_main)
        def _():
            main_ref[...] = x2

        @pl.when(i >= n_main)
        def _():
            tail_ref[...] = x2
    else:
        o_ref, ob_ref = outs
        o_ref[...] = x2
        ob_ref[...] = x2.astype(BF16)


def _combine_ln(x1, y2, route, g, b, layer, alpha, n_p, final):
    n, d = x1.shape
    tm = LN_ROW_TILE
    steps = n // tm
    n_main = n_p // tm
    need = 2 * 4 * tm * d * 4 + 2 * tm * d * 2 + 4 * tm * d * 4
    blk = pl.BlockSpec((tm, d), lambda i: (i, 0))
    vec = pl.BlockSpec((None, 1, d), lambda i: (layer, 0, 0))
    if final:
        out_specs = [pl.BlockSpec((tm, d), lambda i: (jnp.minimum(i, n_main - 1), 0)),
                     pl.BlockSpec((tm, d), lambda i: (jnp.maximum(i - n_main, 0), 0))]
        out_shape = [jax.ShapeDtypeStruct((n_p, d), F32), jax.ShapeDtypeStruct((n - n_p, d), F32)]
    else:
        out_specs = [blk, blk]
        out_shape = [jax.ShapeDtypeStruct((n, d), F32), jax.ShapeDtypeStruct((n, d), BF16)]
    return pl.pallas_call(
        functools.partial(_combine_ln_kernel, alpha=alpha, n_main=n_main, final=final),
        grid=(steps,),
        in_specs=[blk, blk, pl.BlockSpec((tm, d), lambda i: (steps + i, 0)),
                  pl.BlockSpec((tm, LANES), lambda i: (i, 0)), vec, vec],
        out_specs=out_specs,
        out_shape=out_shape,
        compiler_params=_params(("arbitrary",), need + (4 << 20)),
        name="combine_ln",
    )(x1, y2, y2, route, g, b)


def _expert_rows(route, n_experts, tm):
    n_tok = route.shape[0]
    e = route[:, 0:2].astype(jnp.int32).reshape(-1)
    n_pairs = e.shape[0]
    onehot = (e[:, None] == jnp.arange(n_experts, dtype=jnp.int32)[None, :]).astype(jnp.int32)
    running = jnp.cumsum(onehot, axis=0)
    counts = running[-1]
    rank = jnp.sum(running * onehot, axis=1) - 1
    padded = (counts + tm - 1) // tm * tm
    ends = jnp.cumsum(padded)
    starts = ends - padded
    pos = jnp.sum(onehot * starts[None, :], axis=1) + rank
    n_rows = (n_pairs + n_experts * (tm - 1)) // tm * tm
    n_tiles = n_rows // tm
    row_pair = jnp.full((n_rows,), -1, jnp.int32).at[pos].set(jnp.arange(n_pairs, dtype=jnp.int32))
    rows = jnp.arange(n_rows, dtype=jnp.int32)
    real = row_pair >= 0
    row_token = jnp.where(real, row_pair // 2, 0)
    spare = 2 * n_tok + (rows // tm % 2) * tm + rows % tm
    row_dest = jnp.where(real, (row_pair % 2) * n_tok + row_pair // 2, spare)
    n_valid = (ends[-1] // tm).astype(jnp.int32)
    tile_start = jnp.arange(n_tiles, dtype=jnp.int32) * tm
    tile_expert = jnp.sum((tile_start[:, None] >= ends[None, :]).astype(jnp.int32), axis=1)
    last_expert = jnp.max(jnp.where(counts > 0, jnp.arange(n_experts, dtype=jnp.int32), 0))
    tile_expert = jnp.minimum(tile_expert, last_expert).astype(jnp.int32)
    return row_token, row_dest, tile_expert, n_valid.reshape(1)


def kernel(x_prompt, x_sample, cache_k, cache_v, cache_logf, state_conv, state_mlstm_c,
           state_mlstm_n, state_mlstm_m, page_table, w_in, b_in, conv_w, mlstm_norm_w,
           w_branch, w_out, ln1_g, ln1_b, w_router_group, b_router_group, w_router_expert,
           b_router_expert, w_expert_in, w_expert_down, ln2_g, ln2_b):
    bp, t, d = x_prompt.shape
    db = x_sample.shape[0]
    depth = w_in.shape[0]
    n_pool, ps, heads, hd = cache_k.shape[1:]
    aw = heads * hd
    cw = state_conv.shape[-1]
    mh, md = state_mlstm_c.shape[2], state_mlstm_c.shape[3]
    mw = mh * md
    groups = w_router_group.shape[-1]
    n_experts = w_router_expert.shape[-1]
    epg = n_experts // groups
    de = w_expert_down.shape[2]
    alpha = float((2 * depth) ** 0.25)
    n_p = bp * t
    n_tok = n_p + db
    n_all = n_p + ROW_TILE
    moe_tm = MOE_ROW_TILE

    c_f = 3 * aw
    c_i = c_f + heads + 3 * cw + 4 * mw
    g2_cols = 3 * cw + 4 * mw
    g3_cols = 3 * d
    zc_conv = 0
    zc_m = 3 * cw
    zc_g = g2_cols
    i0, f0 = heads, heads + mh
    assert heads + 2 * mh <= LANES and groups + n_experts <= LANES

    x_main = x_prompt.reshape(n_p, d)
    x_tail = jnp.concatenate([x_sample.reshape(db, d), jnp.zeros((ROW_TILE - db, d), F32)], axis=0)
    xb_all = jnp.concatenate([x_main, x_tail], axis=0).astype(BF16)
    b_in3 = b_in.reshape(depth, 1, b_in.shape[1])
    wt_in = jnp.swapaxes(w_in, 1, 2)

    cache_k2 = cache_k.reshape(depth, n_pool, ps * heads, hd)
    cache_v2 = cache_v.reshape(depth, n_pool, ps * heads, hd)
    cache_lft = jnp.swapaxes(cache_logf, 2, 3)
    gain3 = mlstm_norm_w.reshape(depth, 1, mw)
    ln1_g3, ln1_b3 = ln1_g.reshape(depth, 1, d), ln1_b.reshape(depth, 1, d)
    ln2_g3, ln2_b3 = ln2_g.reshape(depth, 1, d), ln2_b.reshape(depth, 1, d)

    def tail(rows):
        rows = rows.astype(BF16)
        return jnp.concatenate([rows, jnp.zeros((ROW_TILE - db, rows.shape[1]), BF16)], axis=0)

    outs = [[] for _ in range(12)]
    k_all = v_all = None
    for l in range(depth):
        b_l = b_in[l]
        b_main = jnp.concatenate([b_l[c_f + heads:c_i], b_l[c_i + 2 * mh:]]).reshape(1, -1)
        b_gate = jnp.concatenate([b_l[c_f:c_f + heads], b_l[c_i:c_i + 2 * mh],
                                  jnp.zeros((LANES - heads - 2 * mh,), F32)]).reshape(1, LANES)
        qb, kvb, k_all, v_all, kv_tail = _qkv_proj(xb_all, wt_in, b_in3, k_all, v_all, l, aw, n_p)
        zc = _main_proj(xb_all, wt_in, b_main, l, c_f, g2_cols, g3_cols, heads, heads + 2 * mh)
        zg = _gate_proj(xb_all, wt_in, b_gate, l, c_f, c_i, heads, mh)

        lf_p, cum, cumt = _prompt_gates(zg, bp, t)
        ya = _prompt_attention(qb, kvb, cum, cumt, n_all, bp, t, heads, hd)
        yc, conv_p = _prompt_conv(zc, conv_w, l, zc_conv, n_all, bp, t, cw)
        ym, c_p, n_p_state, m_p = _prompt_mlstm(zc, zg, gain3, l, zc_m, n_all, bp, t, mh, md, i0, f0)

        zs = zc[n_p:n_tok]
        zgs = zg[n_p:n_tok]
        yc_s, conv_s, ym_s, c_s, n_s, m_s, lf_s = _sample_mix(
            zs, zgs, conv_w, gain3, state_conv, state_mlstm_c, state_mlstm_n, state_mlstm_m,
            l, (zc_conv, zc_m), mh, md, i0, f0)
        q_s = qb[n_p:n_tok].astype(F32).reshape(db, heads, hd)
        k_s = kv_tail[:db, 0:aw].reshape(db, heads, hd)
        v_s = kv_tail[:db, aw:2 * aw].reshape(db, heads, hd)
        lf_new = lf_s[:, 0, 0:heads]
        ya_s = _decode_attention(page_table, q_s, k_s, v_s, lf_new.reshape(db, heads, 1),
                                 cache_k2, cache_v2, cache_lft, l)
        ya = _put_tail_rows(ya, tail(ya_s.reshape(db, aw)))
        yc = _put_tail_rows(yc, tail(yc_s.reshape(db, cw)))
        ym = _put_tail_rows(ym, tail(ym_s.reshape(db, mw)))

        merged = _merge(ya, yc, ym, zc, zc_g, w_branch, l, d)
        y1 = _outproj(merged, x_main, x_tail, w_out, l, alpha)

        wr = jnp.concatenate([w_router_group[l], w_router_expert[l],
                              jnp.zeros((d, LANES - groups - n_experts), F32)], axis=1)
        br = jnp.concatenate([b_router_group[l], b_router_expert[l],
                              jnp.zeros((LANES - groups - n_experts,), F32)]).reshape(1, LANES)
        x1, route = _ln_router(y1, ln1_g3, ln1_b3, wr, br, l, groups, epg)

        row_token, row_dest, tile_expert, n_valid = _expert_rows(route, n_experts, moe_tm)
        xs = _moe_gather(row_token, n_valid, x1, moe_tm)
        act = _moe_in(tile_expert, n_valid, xs, w_expert_in, l, de, moe_tm)
        y2 = _moe_down(tile_expert, n_valid, row_dest, act, w_expert_down, l,
                       2 * n_all + 2 * moe_tm, moe_tm)
        final = l == depth - 1
        x_main, x_next = _combine_ln(x1, y2, route, ln2_g3, ln2_b3, l, alpha, n_p, final)
        if final:
            y_prompt = x_main.reshape(bp, t, d)
            y_sample = x_next[:db].reshape(db, 1, d)
        else:
            x_tail, xb_all = x_main, x_next

        per_layer = (
            lf_p[:, 0:heads].reshape(bp, t, heads),
            k_s.reshape(db, 1, heads, hd),
            v_s.reshape(db, 1, heads, hd),
            lf_new.reshape(db, 1, heads),
            conv_p, conv_s,
            c_p, c_s,
            n_p_state.reshape(bp, mh, md), n_s,
            m_p.reshape(bp, mh), m_s.reshape(db, mh),
        )
        for acc, val in zip(outs, per_layer):
            acc.append(val)

    stacked = [jnp.stack(a, axis=0) for a in outs]
    return (y_prompt, y_sample,
            k_all.reshape(depth, bp, t, heads, hd), v_all.reshape(depth, bp, t, heads, hd),
            *stacked)
```

```python
import functools
import math

import jax
import jax.numpy as jnp
from jax import lax
from jax.experimental import pallas as pl
from jax.experimental.pallas import tpu as pltpu

F32 = jnp.float32
BF16 = jnp.bfloat16
HIGHEST = lax.Precision.HIGHEST
NEG_INF = float("-inf")

LANES = 128
ROW_TILE = 512
LN_ROW_TILE = 256
MOE_ROW_TILE = 256
V7X_VMEM_BYTES = 64 * 1024 * 1024
VMEM_RESERVE_BYTES = 6 * 1024 * 1024
LN_EPS = 1e-5
NT_DIMS = (((1,), (1,)), ((), ()))


def _vmem_limit(need_bytes):
    return int(min(V7X_VMEM_BYTES - VMEM_RESERVE_BYTES, max(need_bytes, 16 * 1024 * 1024)))


def _params(semantics, need_bytes):
    return pltpu.CompilerParams(dimension_semantics=semantics,
                                vmem_limit_bytes=_vmem_limit(need_bytes))


def _pick(n, prefs):
    for p in prefs:
        if n % p == 0:
            return p
    raise ValueError(f"no tile in {prefs} divides {n}")


def _log_sigmoid(x):
    return jnp.minimum(x, 0.0) - jnp.log1p(jnp.exp(-jnp.abs(x)))


def _sigmoid(x):
    return 1.0 / (1.0 + jnp.exp(-x))


def _cast_rows(src_ref, dst_ref, chunk):
    rows = src_ref.shape[0]

    def body(c, carry):
        r = pl.multiple_of(c * chunk, chunk)
        dst_ref[pl.ds(r, chunk), :] = src_ref[pl.ds(r, chunk), :].astype(BF16)
        return carry

    lax.fori_loop(0, rows // chunk, body, 0)


def _qkv_proj_kernel(*refs, chunk, nq, n_main, n_rows, aliased):
    x_ref, w_ref, b_ref = refs[:3]
    outs = refs[3 + (2 if aliased else 0):]
    qb_ref, kvb_ref, k_ref, v_ref, tail_ref, wb_ref = outs
    j = pl.program_id(0)
    i = pl.program_id(1)

    @pl.when(i == 0)
    def _():
        _cast_rows(w_ref, wb_ref, chunk)

    z = lax.dot_general(x_ref[...], wb_ref[...], NT_DIMS, preferred_element_type=F32) + b_ref[...]

    @pl.when(j < nq)
    def _():
        qb_ref[...] = z.astype(BF16)

    @pl.when(j >= nq)
    def _():
        kvb_ref[...] = z.astype(BF16)

    @pl.when(jnp.logical_and(jnp.logical_and(j >= nq, j < 2 * nq), i < n_main))
    def _():
        k_ref[...] = z

    @pl.when(jnp.logical_and(j >= 2 * nq, i < n_main))
    def _():
        v_ref[...] = z

    @pl.when(jnp.logical_and(j >= nq, i == n_rows - 1))
    def _():
        tail_ref[...] = z


def _qkv_proj(xb, wt_in, b_in3, k_all, v_all, layer, aw, n_p):
    n, k = xb.shape
    depth = wt_in.shape[0]
    tm = ROW_TILE
    tn = _pick(aw, (1024, 512, 256, 128))
    nq = aw // tn
    n_rows = n // tm
    n_main = n_p // tm
    aliased = k_all is not None

    def k_map(j, i):
        row = jnp.where(j < nq, 0, jnp.where(j < 2 * nq, jnp.minimum(i, n_main - 1), n_main - 1))
        return (layer, row, jnp.clip(j - nq, 0, nq - 1))

    def v_map(j, i):
        row = jnp.where(j < 2 * nq, 0, jnp.minimum(i, n_main - 1))
        return (layer, row, jnp.clip(j - 2 * nq, 0, nq - 1))

    def q_map(j, i):
        return (jnp.where(j < nq, i, n_rows - 1), jnp.minimum(j, nq - 1))

    def kvb_map(j, i):
        return (jnp.where(j < nq, 0, i), jnp.maximum(j - nq, 0))

    in_specs = [pl.BlockSpec((tm, k), lambda j, i: (i, 0)),
                pl.BlockSpec((None, tn, k), lambda j, i: (layer, j, 0),
                             pipeline_mode=pl.Buffered(1)),
                pl.BlockSpec((None, 1, tn), lambda j, i: (layer, 0, j))]
    args = [xb, wt_in, b_in3]
    aliases = {}
    if aliased:
        in_specs += [pl.BlockSpec(memory_space=pl.ANY), pl.BlockSpec(memory_space=pl.ANY)]
        args += [k_all, v_all]
        aliases = {3: 2, 4: 3}
    need = k * tn * 4 + k * tn * 2 + 2 * tm * k * 2 + 2 * tm * tn * (2 + 2 + 4 + 4 + 4) + 2 * tm * tn * 4
    return pl.pallas_call(
        functools.partial(_qkv_proj_kernel, chunk=_pick(tn, (64, 16)), nq=nq, n_main=n_main,
                          n_rows=n_rows, aliased=aliased),
        grid=(3 * nq, n_rows),
        in_specs=in_specs,
        out_specs=[pl.BlockSpec((tm, tn), q_map),
                   pl.BlockSpec((tm, tn), kvb_map),
                   pl.BlockSpec((None, tm, tn), k_map),
                   pl.BlockSpec((None, tm, tn), v_map),
                   pl.BlockSpec((tm, tn), lambda j, i: (0, jnp.maximum(j - nq, 0)))],
        out_shape=[jax.ShapeDtypeStruct((n, aw), BF16),
                   jax.ShapeDtypeStruct((n, 2 * aw), BF16),
                   jax.ShapeDtypeStruct((depth, n_p, aw), F32),
                   jax.ShapeDtypeStruct((depth, n_p, aw), F32),
                   jax.ShapeDtypeStruct((tm, 2 * aw), F32)],
        scratch_shapes=[pltpu.VMEM((tn, k), BF16)],
        input_output_aliases=aliases,
        compiler_params=_params(("arbitrary", "arbitrary"), need + (4 << 20)),
        name="qkv_proj",
    )(*args)


def _main_proj_kernel(x_ref, w_ref, wn_ref, b_ref, o_ref, wb_ref,
                      *, chunk, g2_tiles, shift2, shift3):
    j = pl.program_id(0)
    i = pl.program_id(1)
    tn = w_ref.shape[0]

    def prepare(shift):
        def body(c, carry):
            r = pl.multiple_of(c * chunk, chunk)
            wb_ref[pl.ds(r, chunk), :] = w_ref[pl.ds(r + shift, chunk), :].astype(BF16)
            return carry

        lax.fori_loop(0, (tn - shift) // chunk, body, 0)
        wb_ref[tn - shift:tn, :] = wn_ref[0:shift, :].astype(BF16)

    @pl.when(jnp.logical_and(i == 0, j < g2_tiles))
    def _():
        prepare(shift2)

    @pl.when(jnp.logical_and(i == 0, j >= g2_tiles))
    def _():
        prepare(shift3)

    acc = lax.dot_general(x_ref[...], wb_ref[...], NT_DIMS, preferred_element_type=F32)
    o_ref[...] = acc + b_ref[...]


def _main_proj(xb, wt_in, b_main, layer, col0, g2_cols, g3_cols, shift2, shift3):
    n, k = xb.shape
    tm = ROW_TILE
    tn = math.gcd(math.gcd(col0, g2_cols), math.gcd(g3_cols, 1024))
    base = col0 // tn
    nb = -(-shift3 // 8) * 8
    assert shift2 <= shift3 and tn % nb == 0
    chunk = math.gcd(math.gcd(shift2, shift3), 64)
    g2_tiles = g2_cols // tn
    n_tiles = (g2_cols + g3_cols) // tn
    need = 2 * k * tn * 4 + 2 * k * nb * 4 + k * tn * 2 + 2 * tm * k * 2 + 2 * tm * tn * 4
    return pl.pallas_call(
        functools.partial(_main_proj_kernel, chunk=chunk, g2_tiles=g2_tiles,
                          shift2=shift2, shift3=shift3),
        grid=(n_tiles, n // tm),
        in_specs=[pl.BlockSpec((tm, k), lambda j, i: (i, 0)),
                  pl.BlockSpec((None, tn, k), lambda j, i: (layer, base + j, 0)),
                  pl.BlockSpec((None, nb, k), lambda j, i: (layer, (base + j + 1) * (tn // nb), 0)),
                  pl.BlockSpec((1, tn), lambda j, i: (0, j))],
        out_specs=pl.BlockSpec((tm, tn), lambda j, i: (i, j)),
        out_shape=jax.ShapeDtypeStruct((n, g2_cols + g3_cols), F32),
        scratch_shapes=[pltpu.VMEM((tn, k), BF16)],
        compiler_params=_params(("arbitrary", "arbitrary"), need + (4 << 20)),
        name="main_proj",
    )(xb, wt_in, wt_in, b_main)


def _gate_proj_kernel(x_ref, wf_ref, wi_ref, b_ref, o_ref, wb_ref):
    nf = wf_ref.shape[0]
    ni = wi_ref.shape[0]

    @pl.when(pl.program_id(0) == 0)
    def _():
        wb_ref[...] = jnp.zeros_like(wb_ref)
        wb_ref[0:nf, :] = wf_ref[...].astype(BF16)
        wb_ref[nf:nf + ni, :] = wi_ref[...].astype(BF16)

    acc = lax.dot_general(x_ref[...], wb_ref[...], NT_DIMS, preferred_element_type=F32)
    o_ref[...] = acc + b_ref[...]


def _gate_proj(xb, wt_in, b_gate, layer, c_f, c_i, heads, mh):
    n, k = xb.shape
    tm = ROW_TILE
    assert c_f % heads == 0 and c_i % (2 * mh) == 0
    need = 4 * (heads + 2 * mh) * k * 4 + LANES * k * 2 + 2 * tm * k * 2 + 3 * tm * LANES * 4
    return pl.pallas_call(
        _gate_proj_kernel,
        grid=(n // tm,),
        in_specs=[pl.BlockSpec((tm, k), lambda i: (i, 0)),
                  pl.BlockSpec((None, heads, k), lambda i: (layer, c_f // heads, 0)),
                  pl.BlockSpec((None, 2 * mh, k), lambda i: (layer, c_i // (2 * mh), 0)),
                  pl.BlockSpec((1, LANES), lambda i: (0, 0))],
        out_specs=pl.BlockSpec((tm, LANES), lambda i: (i, 0)),
        out_shape=jax.ShapeDtypeStruct((n, LANES), F32),
        scratch_shapes=[pltpu.VMEM((LANES, k), BF16)],
        compiler_params=_params(("arbitrary",), need + (4 << 20)),
        name="gate_proj",
    )(xb, wt_in, wt_in, b_gate)


def _merge_kernel(ya_ref, yc_ref, ym_ref, ga_ref, gc_ref, gm_ref, w_ref, o_ref, wb_ref,
                  *, chunk, aw, cw):
    @pl.when(pl.program_id(1) == 0)
    def _():
        _cast_rows(w_ref, wb_ref, chunk)

    da = jnp.dot(ya_ref[...], wb_ref[0:aw, :], preferred_element_type=F32)
    dc = jnp.dot(yc_ref[...], wb_ref[aw:aw + cw, :], preferred_element_type=F32)
    dm = jnp.dot(ym_ref[...], wb_ref[aw + cw:, :], preferred_element_type=F32)
    out = _sigmoid(ga_ref[...]) * da + _sigmoid(gc_ref[...]) * dc + _sigmoid(gm_ref[...]) * dm
    o_ref[...] = out.astype(o_ref.dtype)


def _merge(ya, yc, ym, zc, g_col0, w_branch_l, layer, d):
    n = ya.shape[0]
    aw, cw, mw = ya.shape[1], yc.shape[1], ym.shape[1]
    mix = aw + cw + mw
    tm = ROW_TILE
    tn = _pick(d, (512, 256, 128))
    gb = g_col0 // tn
    nd = d // tn
    need = mix * tn * 4 + mix * tn * 2 + 2 * tm * mix * 2 + 10 * tm * tn * 4
    return pl.pallas_call(
        functools.partial(_merge_kernel, chunk=_pick(mix, (256, 128, 8)), aw=aw, cw=cw),
        grid=(nd, n // tm),
        in_specs=[pl.BlockSpec((tm, aw), lambda j, i: (i, 0)),
                  pl.BlockSpec((tm, cw), lambda j, i: (i, 0)),
                  pl.BlockSpec((tm, mw), lambda j, i: (i, 0)),
                  pl.BlockSpec((tm, tn), lambda j, i: (i, gb + j)),
                  pl.BlockSpec((tm, tn), lambda j, i: (i, gb + nd + j)),
                  pl.BlockSpec((tm, tn), lambda j, i: (i, gb + 2 * nd + j)),
                  pl.BlockSpec((None, mix, tn), lambda j, i: (layer, 0, j),
                               pipeline_mode=pl.Buffered(1))],
        out_specs=pl.BlockSpec((tm, tn), lambda j, i: (i, j)),
        out_shape=jax.ShapeDtypeStruct((n, d), BF16),
        scratch_shapes=[pltpu.VMEM((mix, tn), BF16)],
        compiler_params=_params(("arbitrary", "arbitrary"), need + (4 << 20)),
        name="merge",
    )(ya, yc, ym, zc, zc, zc, w_branch_l)


def _outproj_kernel(m_ref, x_ref, xt_ref, w_ref, o_ref, wb_ref, *, chunk, alpha, n_main):
    i = pl.program_id(1)

    @pl.when(i == 0)
    def _():
        _cast_rows(w_ref, wb_ref, chunk)

    acc = jnp.dot(m_ref[...], wb_ref[...], preferred_element_type=F32)

    @pl.when(i < n_main)
    def _():
        o_ref[...] = alpha * x_ref[...] + acc

    @pl.when(i >= n_main)
    def _():
        o_ref[...] = alpha * xt_ref[...] + acc


def _outproj(mb, x_main, x_tail, w_out, layer, alpha):
    n, k = mb.shape
    d = w_out.shape[2]
    tm = ROW_TILE
    tn = _pick(d, (512, 256, 128))
    n_main = min(x_main.shape[0] // tm, n // tm)
    tail_row = x_tail.shape[0] // tm - 1
    need = 2 * k * tn * 4 + k * tn * 2 + 2 * tm * k * 2 + 8 * tm * tn * 4
    return pl.pallas_call(
        functools.partial(_outproj_kernel, chunk=_pick(k, (256, 128, 8)), alpha=alpha, n_main=n_main),
        grid=(d // tn, n // tm),
        in_specs=[pl.BlockSpec((tm, k), lambda j, i: (i, 0)),
                  pl.BlockSpec((tm, tn), lambda j, i: (jnp.minimum(i, n_main - 1), j)),
                  pl.BlockSpec((tm, tn), lambda j, i: (tail_row, j)),
                  pl.BlockSpec((None, k, tn), lambda j, i: (layer, 0, j))],
        out_specs=pl.BlockSpec((tm, tn), lambda j, i: (i, j)),
        out_shape=jax.ShapeDtypeStruct((n, d), F32),
        scratch_shapes=[pltpu.VMEM((k, tn), BF16)],
        compiler_params=_params(("arbitrary", "arbitrary"), need + (4 << 20)),
        name="outproj",
    )(mb, x_main, x_tail, w_out)


def _gates_kernel(zg_ref, lf_ref, cum_ref, cumt_ref, carry_ref):
    tt = zg_ref.shape[0]

    @pl.when(pl.program_id(1) == 0)
    def _():
        carry_ref[...] = jnp.zeros_like(carry_ref)

    lf = _log_sigmoid(zg_ref[...])
    row = lax.broadcasted_iota(jnp.int32, (tt, tt), 0)
    col = lax.broadcasted_iota(jnp.int32, (tt, tt), 1)
    tri = (row >= col).astype(F32)
    cum = jnp.dot(tri, lf, precision=HIGHEST, preferred_element_type=F32) + carry_ref[...]
    carry_ref[...] = cum[tt - 1:tt, :]
    lf_ref[...] = lf
    cum_ref[...] = cum
    cumt_ref[0] = cum.T


def _prompt_gates(zg, b, t):
    tt = _pick(t, (256, 128))
    nt = t // tt
    return pl.pallas_call(
        _gates_kernel,
        grid=(b, nt),
        in_specs=[pl.BlockSpec((tt, LANES), lambda bi, ti: (bi * nt + ti, 0))],
        out_specs=[pl.BlockSpec((tt, LANES), lambda bi, ti: (bi * nt + ti, 0)),
                   pl.BlockSpec((tt, LANES), lambda bi, ti: (bi * nt + ti, 0)),
                   pl.BlockSpec((1, LANES, tt), lambda bi, ti: (bi, 0, ti))],
        out_shape=[jax.ShapeDtypeStruct((b * t, LANES), F32),
                   jax.ShapeDtypeStruct((b * t, LANES), F32),
                   jax.ShapeDtypeStruct((b, LANES, t), F32)],
        scratch_shapes=[pltpu.VMEM((1, LANES), F32)],
        compiler_params=_params(("arbitrary", "arbitrary"), 0),
        name="prompt_gates",
    )(zg)


def _attn_kernel(q_ref, k_ref, v_ref, cq_ref, ck_ref, o_ref, m_ref, l_ref, acc_ref,
                 *, heads, hd, scale, qs):
    qi = pl.program_id(1)
    ki = pl.program_id(2)
    tq = q_ref.shape[0]
    tk = k_ref.shape[0]

    @pl.when(ki == 0)
    def _():
        m_ref[...] = jnp.full(m_ref.shape, -1e30, F32)
        l_ref[...] = jnp.zeros_like(l_ref)
        acc_ref[...] = jnp.zeros_like(acc_ref)

    def block(diagonal):
        cq = cq_ref[...]
        ck = ck_ref[0]
        for h in range(heads):
            sl = slice(h * hd, (h + 1) * hd)
            kh = k_ref[:, sl]
            vh = v_ref[:, sl]
            ckh = ck[h:h + 1, :]
            for r in range(tq // qs):
                rs = slice(r * qs, (r + 1) * qs)
                s = lax.dot_general(q_ref[rs, sl], kh, NT_DIMS, preferred_element_type=F32) * scale
                s = s + (cq[rs, h:h + 1] - ckh)
                if diagonal:
                    row = lax.broadcasted_iota(jnp.int32, (qs, tk), 0) + r * qs
                    col = lax.broadcasted_iota(jnp.int32, (qs, tk), 1)
                    s = jnp.where(col <= row, s, -1e30)
                m_old = m_ref[h, rs]
                m_new = jnp.maximum(m_old, jnp.max(s, axis=1, keepdims=True))
                p = jnp.exp(s - m_new)
                a = jnp.exp(m_old - m_new)
                l_ref[h, rs] = a * l_ref[h, rs] + jnp.sum(p, axis=1, keepdims=True)
                acc_ref[rs, sl] = a * acc_ref[rs, sl] + jnp.dot(p.astype(BF16), vh,
                                                                preferred_element_type=F32)
                m_ref[h, rs] = m_new

    @pl.when(ki < qi)
    def _():
        block(False)

    @pl.when(ki == qi)
    def _():
        block(True)
        for h in range(heads):
            sl = slice(h * hd, (h + 1) * hd)
            o_ref[:, sl] = (acc_ref[:, sl] / l_ref[h]).astype(o_ref.dtype)


def _prompt_attention(qb, kvb, cum, cumt, n_all, b, t, heads, hd):
    aw = heads * hd
    tq = _pick(t, (256, 128))
    nt = t // tq
    need = 2 * 3 * tq * aw * 2 + 2 * tq * aw * 2 + tq * aw * 4 + 2 * heads * tq * LANES * 4
    return pl.pallas_call(
        functools.partial(_attn_kernel, heads=heads, hd=hd, scale=hd ** -0.5, qs=tq),
        grid=(b, nt, nt),
        in_specs=[pl.BlockSpec((tq, aw), lambda bi, qi, ki: (bi * nt + qi, 0)),
                  pl.BlockSpec((tq, aw), lambda bi, qi, ki: (bi * nt + jnp.minimum(ki, qi), 0)),
                  pl.BlockSpec((tq, aw), lambda bi, qi, ki: (bi * nt + jnp.minimum(ki, qi), 1)),
                  pl.BlockSpec((tq, LANES), lambda bi, qi, ki: (bi * nt + qi, 0)),
                  pl.BlockSpec((1, LANES, tq), lambda bi, qi, ki: (bi, 0, jnp.minimum(ki, qi)))],
        out_specs=pl.BlockSpec((tq, aw), lambda bi, qi, ki: (bi * nt + qi, 0)),
        out_shape=jax.ShapeDtypeStruct((n_all, aw), BF16),
        scratch_shapes=[pltpu.VMEM((heads, tq, 1), F32),
                        pltpu.VMEM((heads, tq, 1), F32),
                        pltpu.VMEM((tq, aw), F32)],
        compiler_params=_params(("arbitrary", "arbitrary", "arbitrary"), need + (8 << 20)),
        name="prompt_attention",
    )(qb, kvb, kvb, cum, cumt)


def _decode_kernel(pt_ref, q_ref, kn_ref, vn_ref, lfn_ref, *rest, pps, heads, scale):
    del pt_ref
    k_refs = rest[:pps]
    v_refs = rest[pps:2 * pps]
    lf_refs = rest[2 * pps:3 * pps]
    o_ref = rest[3 * pps]
    m_ref, l_ref, acc_ref, carry_ref = rest[3 * pps + 1:]
    p_idx = pl.program_id(1)
    ps = lf_refs[0].shape[1]
    hd = q_ref.shape[2]
    q = q_ref[0]

    @pl.when(p_idx == 0)
    def _():
        m_ref[...] = jnp.sum(q * kn_ref[0], axis=1, keepdims=True) * scale
        l_ref[...] = jnp.ones_like(l_ref)
        acc_ref[...] = vn_ref[0]
        carry_ref[...] = lfn_ref[0]

    q8 = [jnp.broadcast_to(q[h:h + 1, :], (8, hd)).astype(BF16) for h in range(heads)]
    row = lax.broadcasted_iota(jnp.int32, (ps, ps), 0)
    col = lax.broadcasted_iota(jnp.int32, (ps, ps), 1)
    later = (row > col).astype(F32)
    for i in range(pps):
        lf = lf_refs[i][...]
        bias = jnp.dot(lf, later, precision=HIGHEST, preferred_element_type=F32) + carry_ref[...]
        carry_ref[...] = carry_ref[...] + jnp.sum(lf, axis=1, keepdims=True)
        rows = []
        for h in range(heads):
            kh = k_refs[i][pl.ds(h, ps, stride=heads), :].astype(BF16)
            rows.append(lax.dot_general(q8[h], kh, NT_DIMS, preferred_element_type=F32)[0:1, :])
        s = jnp.concatenate(rows, axis=0) * scale + bias
        m_old = m_ref[...]
        m_new = jnp.maximum(m_old, jnp.max(s, axis=1, keepdims=True))
        pr = jnp.exp(s - m_new)
        a = jnp.exp(m_old - m_new)
        l_ref[...] = a * l_ref[...] + jnp.sum(pr, axis=1, keepdims=True)
        outs = []
        for h in range(heads):
            ph = jnp.broadcast_to(pr[h:h + 1, :], (8, ps)).astype(BF16)
            vh = v_refs[i][pl.ds(h, ps, stride=heads), :].astype(BF16)
            outs.append(jnp.dot(ph, vh, preferred_element_type=F32)[0:1, :])
        acc_ref[...] = a * acc_ref[...] + jnp.concatenate(outs, axis=0)
        m_ref[...] = m_new

    @pl.when(p_idx == pl.num_programs(1) - 1)
    def _():
        o_ref[0] = acc_ref[...] / l_ref[...]


def _decode_attention(page_table, q, k_new, v_new, lf_new, cache_k2, cache_v2, cache_lft, layer):
    db, heads, hd = q.shape
    n_pages = page_table.shape[1]
    ps = cache_lft.shape[3]
    pps = _pick(n_pages, (4, 2, 1))
    steps = n_pages // pps
    pt_flat = page_table.reshape(-1)

    def page_map(i):
        def index_map(b, p, pt):
            return (layer, pt[b * n_pages + (n_pages - 1 - (p * pps + i))], 0, 0)
        return index_map

    small = lambda b, p, pt: (b, 0, 0)
    in_specs = [pl.BlockSpec((1, heads, hd), small), pl.BlockSpec((1, heads, hd), small),
                pl.BlockSpec((1, heads, hd), small), pl.BlockSpec((1, heads, 1), small)]
    in_specs += [pl.BlockSpec((None, None, ps * heads, hd), page_map(i)) for i in range(pps)]
    in_specs += [pl.BlockSpec((None, None, ps * heads, hd), page_map(i)) for i in range(pps)]
    in_specs += [pl.BlockSpec((None, None, heads, ps), page_map(i)) for i in range(pps)]
    need = 2 * 2 * pps * ps * heads * hd * 4
    grid_spec = pltpu.PrefetchScalarGridSpec(
        num_scalar_prefetch=1,
        grid=(db, steps),
        in_specs=in_specs,
        out_specs=pl.BlockSpec((1, heads, hd), small),
        scratch_shapes=[pltpu.VMEM((heads, 1), F32), pltpu.VMEM((heads, 1), F32),
                        pltpu.VMEM((heads, hd), F32), pltpu.VMEM((heads, 1), F32)],
    )
    return pl.pallas_call(
        functools.partial(_decode_kernel, pps=pps, heads=heads, scale=hd ** -0.5),
        grid_spec=grid_spec,
        out_shape=jax.ShapeDtypeStruct((db, heads, hd), F32),
        compiler_params=_params(("arbitrary", "arbitrary"), need + (8 << 20)),
        name="decode_attention",
    )(pt_flat, q, k_new, v_new, lf_new,
      *([cache_k2] * pps), *([cache_v2] * pps), *([cache_lft] * pps))


def _conv_kernel(cb_ref, cc_ref, ch_ref, w_ref, y_ref, cn_ref):
    t = cc_ref.shape[0]
    u = cc_ref[...] * ch_ref[...]
    row = lax.broadcasted_iota(jnp.int32, u.shape, 0)
    u1 = jnp.where(row >= 1, pltpu.roll(u, 1, axis=0), 0.0)
    u2 = jnp.where(row >= 2, pltpu.roll(u, 2, axis=0), 0.0)
    w = w_ref[...]
    y = cb_ref[...] * (w[0:1, :] * u2 + w[1:2, :] * u1 + w[2:3, :] * u)
    y_ref[...] = y.astype(y_ref.dtype)
    cn_ref[0] = u[t - 2:t, :]


def _prompt_conv(zc, conv_w, layer, col0, n_all, b, t, cw):
    tc = _pick(cw, (256, 128))
    cb0 = col0 // tc
    nc = cw // tc
    kk = conv_w.shape[1]
    need = 2 * 3 * t * tc * 4 + 2 * t * tc * 2 + 6 * t * tc * 4
    return pl.pallas_call(
        _conv_kernel,
        grid=(b, nc),
        in_specs=[pl.BlockSpec((t, tc), lambda bi, j: (bi, cb0 + j)),
                  pl.BlockSpec((t, tc), lambda bi, j: (bi, cb0 + nc + j)),
                  pl.BlockSpec((t, tc), lambda bi, j: (bi, cb0 + 2 * nc + j)),
                  pl.BlockSpec((None, kk, tc), lambda bi, j: (layer, 0, j))],
        out_specs=[pl.BlockSpec((t, tc), lambda bi, j: (bi, j)),
                   pl.BlockSpec((1, kk - 1, tc), lambda bi, j: (bi, 0, j))],
        out_shape=[jax.ShapeDtypeStruct((n_all, cw), BF16),
                   jax.ShapeDtypeStruct((b, kk - 1, cw), F32)],
        compiler_params=_params(("arbitrary", "arbitrary"), need + (4 << 20)),
        name="prompt_conv",
    )(zc, zc, zc, conv_w)


def _head_norm(x, gain_row):
    mu = jnp.mean(x, axis=1, keepdims=True)
    xc = x - mu
    var = jnp.mean(xc * xc, axis=1, keepdims=True)
    return xc * lax.rsqrt(var + LN_EPS) * gain_row


def _mlstm_kernel(q_ref, k_ref, v_ref, o_ref, zg_ref, gain_ref, y_ref, c_out, n_out, m_out,
                  c_ref, n_ref, m_ref, *, heads, md, i0, f0):
    ci = pl.program_id(1)
    lc = q_ref.shape[0]

    @pl.when(ci == 0)
    def _():
        c_ref[...] = jnp.zeros_like(c_ref)
        n_ref[...] = jnp.zeros_like(n_ref)
        m_ref[...] = jnp.zeros_like(m_ref)

    zg = zg_ref[...]
    zgt = zg.T
    row = lax.broadcasted_iota(jnp.int32, (lc, lc), 0)
    col = lax.broadcasted_iota(jnp.int32, (lc, lc), 1)
    causal = col <= row
    bcols = jnp.dot(causal.astype(F32), _log_sigmoid(zg), precision=HIGHEST,
                    preferred_element_type=F32)
    brows = bcols.T
    kscale = md ** -0.5
    for h in range(heads):
        sl = slice(h * md, (h + 1) * md)
        bcol = bcols[:, f0 + h:f0 + h + 1]
        brow = brows[f0 + h:f0 + h + 1, :]
        icol = zg[:, i0 + h:i0 + h + 1]
        irow = zgt[i0 + h:i0 + h + 1, :]
        m_prev = m_ref[h]
        cst = c_ref[h]
        nst = n_ref[h]
        qh = q_ref[:, sl]
        kh = k_ref[:, sl] * kscale
        qb = qh.astype(BF16)
        kb = kh.astype(BF16)
        vb = v_ref[:, sl].astype(BF16)

        dmat = jnp.where(causal, bcol - brow + irow, NEG_INF)
        inter = bcol + m_prev
        m_t = jnp.maximum(jnp.max(dmat, axis=1, keepdims=True), inter)
        w_intra = jnp.exp(dmat - m_t)
        w_inter = jnp.exp(inter - m_t)
        s = lax.dot_general(qb, kb, NT_DIMS, preferred_element_type=F32) * w_intra
        num = jnp.dot(s.astype(BF16), vb, preferred_element_type=F32) + w_inter * lax.dot_general(
            qb, cst.astype(BF16), NT_DIMS, preferred_element_type=F32)
        den = jnp.sum(s, axis=1, keepdims=True) + w_inter * jnp.sum(qh * nst, axis=1, keepdims=True)
        hh = num / jnp.maximum(jnp.abs(den), jnp.exp(-m_t))
        og = _sigmoid(o_ref[:, sl])
        y_ref[:, sl] = _head_norm(og * hh, gain_ref[:, sl]).astype(y_ref.dtype)

        g = bcol[lc - 1:lc, :]
        acol = g - bcol + icol
        m_new = jnp.maximum(g + m_prev, jnp.max(acol, axis=0, keepdims=True))
        decay = jnp.exp(g + m_prev - m_new)
        wa = jnp.exp(acol - m_new)
        vwt = (v_ref[:, sl] * wa).T.astype(BF16)
        c_ref[h] = decay * cst + jnp.dot(vwt, kb, preferred_element_type=F32)
        n_ref[h] = decay * nst + jnp.sum(kh * wa, axis=0, keepdims=True)
        m_ref[h] = m_new

    @pl.when(ci == pl.num_programs(1) - 1)
    def _():
        c_out[0] = c_ref[...]
        n_out[0] = n_ref[...]
        m_out[0] = m_ref[...]


def _prompt_mlstm(zc, zg, gain, layer, col0, n_all, b, t, heads, md, i0, f0):
    mw = heads * md
    lc = _pick(t, (256, 128))
    nch = t // lc
    assert col0 % mw == 0
    qb0 = col0 // mw
    need = 2 * 4 * lc * mw * 4 + 2 * lc * mw * 2 + 3 * heads * md * md * 4 + 12 * lc * lc * 4
    zblock = lambda off: pl.BlockSpec((lc, mw), lambda bi, ci: (bi * nch + ci, qb0 + off))
    return pl.pallas_call(
        functools.partial(_mlstm_kernel, heads=heads, md=md, i0=i0, f0=f0),
        grid=(b, nch),
        in_specs=[zblock(0), zblock(1), zblock(2), zblock(3),
                  pl.BlockSpec((lc, LANES), lambda bi, ci: (bi * nch + ci, 0)),
                  pl.BlockSpec((None, 1, mw), lambda bi, ci: (layer, 0, 0))],
        out_specs=[pl.BlockSpec((lc, mw), lambda bi, ci: (bi * nch + ci, 0)),
                   pl.BlockSpec((1, heads, md, md), lambda bi, ci: (bi, 0, 0, 0)),
                   pl.BlockSpec((1, heads, 1, md), lambda bi, ci: (bi, 0, 0, 0)),
                   pl.BlockSpec((1, heads, 1, 1), lambda bi, ci: (bi, 0, 0, 0))],
        out_shape=[jax.ShapeDtypeStruct((n_all, mw), BF16),
                   jax.ShapeDtypeStruct((b, heads, md, md), F32),
                   jax.ShapeDtypeStruct((b, heads, 1, md), F32),
                   jax.ShapeDtypeStruct((b, heads, 1, 1), F32)],
        scratch_shapes=[pltpu.VMEM((heads, md, md), F32),
                        pltpu.VMEM((heads, 1, md), F32),
                        pltpu.VMEM((heads, 1, 1), F32)],
        compiler_params=_params(("arbitrary", "arbitrary"), need + (8 << 20)),
        name="prompt_mlstm",
    )(zc, zc, zc, zc, zg, gain)


def _sample_mix_kernel(cb_ref, cc_ref, ch_ref, cw_ref, prev_ref, q_ref, k_ref, v_ref, vcol_ref,
                       o_ref, zg_ref, gain_ref, c0_ref, n0_ref, m0_ref,
                       yc_ref, cn_ref, ym_ref, c1_ref, n1_ref, m1_ref, lf_ref,
                       *, heads, md, i0, f0):
    u = cc_ref[0] * ch_ref[0]
    prev = prev_ref[0]
    w = cw_ref[...]
    yc_ref[0] = cb_ref[0] * (w[0:1, :] * prev[0:1, :] + w[1:2, :] * prev[1:2, :] + w[2:3, :] * u)
    cn_ref[0, 0:1, :] = prev[1:2, :]
    cn_ref[0, 1:2, :] = u

    zg = zg_ref[0]
    lfz = _log_sigmoid(zg)
    lf_ref[0] = lfz
    lane = lax.broadcasted_iota(jnp.int32, (1, heads), 1)
    m_row = jnp.zeros((1, heads), F32)
    kscale = md ** -0.5
    for h in range(heads):
        sl = slice(h * md, (h + 1) * md)
        qh = q_ref[0][:, sl]
        kh = k_ref[0][:, sl] * kscale
        vh = v_ref[0][:, sl]
        ig = zg[:, i0 + h:i0 + h + 1]
        fg = lfz[:, f0 + h:f0 + h + 1]
        m_prev = m0_ref[0][:, h:h + 1]
        cst = c0_ref[0, h]
        nst = n0_ref[0][h:h + 1, :]
        inter = fg + m_prev
        m_t = jnp.maximum(ig, inter)
        w_intra = jnp.exp(ig - m_t)
        w_inter = jnp.exp(inter - m_t)
        s = jnp.sum(qh * kh, axis=1, keepdims=True) * w_intra
        q8 = jnp.broadcast_to(qh, (8, md))
        cq = lax.dot_general(q8, cst, NT_DIMS, precision=HIGHEST,
                             preferred_element_type=F32)[0:1, :]
        num = s * vh + w_inter * cq
        den = s + w_inter * jnp.sum(nst * qh, axis=1, keepdims=True)
        hh = num / jnp.maximum(jnp.abs(den), jnp.exp(-m_t))
        og = _sigmoid(o_ref[0][:, sl])
        ym_ref[0, :, sl] = _head_norm(og * hh, gain_ref[:, sl])
        m_new = jnp.maximum(inter, ig)
        decay = jnp.exp(inter - m_new)
        wa = jnp.exp(ig - m_new)
        c1_ref[0, h] = decay * cst + (vcol_ref[0, h] * wa) * kh
        n1_ref[0, h:h + 1, :] = decay * nst + wa * kh
        m_row = jnp.where(lane == h, m_new, m_row)
    m1_ref[0] = m_row


def _sample_mix(zs, zgs, conv_w, gain, state_conv, c0, n0, m0, layer, cols, heads, md, i0, f0):
    db = zs.shape[0]
    cw = state_conv.shape[-1]
    mw = heads * md
    kk = conv_w.shape[1]
    cb0, q0 = cols
    seg = lambda a, width: zs[:, a:a + width].reshape(db, 1, width)
    cb, cc, ch = seg(cb0, cw), seg(cb0 + cw, cw), seg(cb0 + 2 * cw, cw)
    q, k, v, o = (seg(q0 + i * mw, mw) for i in range(4))
    vcol = v.reshape(db, heads, md, 1)
    zg3 = zgs.reshape(db, 1, LANES)
    m03 = m0.reshape(m0.shape[0], db, 1, heads)
    row3 = lambda width: pl.BlockSpec((1, 1, width), lambda b: (b, 0, 0))
    in_specs = [row3(cw), row3(cw), row3(cw),
                pl.BlockSpec((None, kk, cw), lambda b: (layer, 0, 0)),
                pl.BlockSpec((None, 1, kk - 1, cw), lambda b: (layer, b, 0, 0)),
                row3(mw), row3(mw), row3(mw),
                pl.BlockSpec((1, heads, md, 1), lambda b: (b, 0, 0, 0)),
                row3(mw), row3(LANES),
                pl.BlockSpec((None, 1, mw), lambda b: (layer, 0, 0)),
                pl.BlockSpec((None, 1, heads, md, md), lambda b: (layer, b, 0, 0, 0)),
                pl.BlockSpec((None, 1, heads, md), lambda b: (layer, b, 0, 0)),
                pl.BlockSpec((None, 1, 1, heads), lambda b: (layer, b, 0, 0))]
    out_specs = [row3(cw),
                 pl.BlockSpec((1, kk - 1, cw), lambda b: (b, 0, 0)),
                 row3(mw),
                 pl.BlockSpec((1, heads, md, md), lambda b: (b, 0, 0, 0)),
                 pl.BlockSpec((1, heads, md), lambda b: (b, 0, 0)),
                 pl.BlockSpec((1, 1, heads), lambda b: (b, 0, 0)),
                 row3(LANES)]
    out_shape = [jax.ShapeDtypeStruct((db, 1, cw), F32),
                 jax.ShapeDtypeStruct((db, kk - 1, cw), F32),
                 jax.ShapeDtypeStruct((db, 1, mw), F32),
                 jax.ShapeDtypeStruct((db, heads, md, md), F32),
                 jax.ShapeDtypeStruct((db, heads, md), F32),
                 jax.ShapeDtypeStruct((db, 1, heads), F32),
                 jax.ShapeDtypeStruct((db, 1, LANES), F32)]
    need = 4 * heads * md * md * 4 + 2 * heads * md * LANES * 4
    return pl.pallas_call(
        functools.partial(_sample_mix_kernel, heads=heads, md=md, i0=i0, f0=f0),
        grid=(db,),
        in_specs=in_specs,
        out_specs=out_specs,
        out_shape=out_shape,
        compiler_params=_params(("arbitrary",), need + (8 << 20)),
        name="sample_mix",
    )(cb, cc, ch, conv_w, state_conv, q, k, v, vcol, o, zg3, gain, c0, n0, m03)


def _put_rows_kernel(dst_ref, rows_ref, o_ref):
    del dst_ref
    o_ref[...] = rows_ref[...]


def _put_tail_rows(dst, rows):
    n, w = dst.shape
    last = n // ROW_TILE - 1
    return pl.pallas_call(
        _put_rows_kernel,
        grid=(1,),
        in_specs=[pl.BlockSpec(memory_space=pl.ANY),
                  pl.BlockSpec((ROW_TILE, w), lambda i: (0, 0))],
        out_specs=pl.BlockSpec((ROW_TILE, w), lambda i: (last, 0)),
        out_shape=jax.ShapeDtypeStruct(dst.shape, dst.dtype),
        input_output_aliases={0: 0},
        name="put_tail_rows",
    )(dst, rows)


def _layer_norm_rows(y, g, b):
    mu = jnp.mean(y, axis=1, keepdims=True)
    yc = y - mu
    var = jnp.mean(yc * yc, axis=1, keepdims=True)
    return yc * lax.rsqrt(var + LN_EPS) * g + b


def _first_argmax(x, width):
    lane = lax.broadcasted_iota(jnp.int32, x.shape, 1).astype(F32)
    mx = jnp.max(x, axis=1, keepdims=True)
    idx = jnp.min(jnp.where(x == mx, lane, float(width)), axis=1, keepdims=True)
    return mx, idx


def _ln_router_kernel(y_ref, g_ref, b_ref, wr_ref, br_ref, x_ref, r_ref, *, groups, epg):
    x1 = _layer_norm_rows(y_ref[...], g_ref[...], b_ref[...])
    x_ref[...] = x1
    w = wr_ref[...]
    xh = x1.astype(BF16)
    xl = (x1 - xh.astype(F32)).astype(BF16)
    wh = w.astype(BF16)
    wl = (w - wh.astype(F32)).astype(BF16)
    logits = (jnp.dot(xh, wh, preferred_element_type=F32) + jnp.dot(xh, wl, preferred_element_type=F32)
              + jnp.dot(xl, wh, preferred_element_type=F32) + br_ref[...])
    gl = logits[:, 0:groups]
    gmax, gstar = _first_argmax(gl, groups)
    p_star = 1.0 / jnp.sum(jnp.exp(gl - gmax), axis=1, keepdims=True)
    elg = logits[:, groups:groups + epg]
    for g in range(1, groups):
        elg = jnp.where(gstar == g, logits[:, groups + g * epg:groups + (g + 1) * epg], elg)
    v0, i0 = _first_argmax(elg, epg)
    lane = lax.broadcasted_iota(jnp.int32, elg.shape, 1).astype(F32)
    v1, i1 = _first_argmax(jnp.where(lane == i0, NEG_INF, elg), epg)
    e1 = jnp.exp(v1 - v0)
    w0 = p_star / (1.0 + e1)
    w1 = p_star * e1 / (1.0 + e1)
    out_lane = lax.broadcasted_iota(jnp.int32, r_ref.shape, 1)
    ex0 = (gstar * epg + i0).astype(F32)
    ex1 = (gstar * epg + i1).astype(F32)
    r = jnp.where(out_lane == 0, ex0, 0.0)
    r = jnp.where(out_lane == 1, ex1, r)
    r = jnp.where(out_lane == 2, w0, r)
    r = jnp.where(out_lane == 3, w1, r)
    r_ref[...] = r


def _ln_router(y, g, b, wr, br, layer, groups, epg):
    n, d = y.shape
    tm = LN_ROW_TILE
    need = 2 * tm * d * 4 * 2 + 2 * d * LANES * 4 + 6 * tm * d * 4
    vec = pl.BlockSpec((None, 1, d), lambda i: (layer, 0, 0))
    return pl.pallas_call(
        functools.partial(_ln_router_kernel, groups=groups, epg=epg),
        grid=(n // tm,),
        in_specs=[pl.BlockSpec((tm, d), lambda i: (i, 0)), vec, vec,
                  pl.BlockSpec((d, LANES), lambda i: (0, 0)),
                  pl.BlockSpec((1, LANES), lambda i: (0, 0))],
        out_specs=[pl.BlockSpec((tm, d), lambda i: (i, 0)),
                   pl.BlockSpec((tm, LANES), lambda i: (i, 0))],
        out_shape=[jax.ShapeDtypeStruct((n, d), F32),
                   jax.ShapeDtypeStruct((n, LANES), F32)],
        compiler_params=_params(("arbitrary",), need + (4 << 20)),
        name="ln_router",
    )(y, g, b, wr, br)


def _moe_gather_kernel(tok_ref, nv_ref, x_hbm, o_ref, buf_ref, sem_ref, *, tm):
    t = pl.program_id(0)
    nv = nv_ref[0]
    slot = t % 2

    def row_copy(tok, k, s):
        return pltpu.make_async_copy(x_hbm.at[pl.ds(tok, 1), :], buf_ref.at[s, pl.ds(k, 1), :],
                                     sem_ref.at[s])

    def issue(tile, s):
        def body(k, carry):
            row_copy(tok_ref[tile * tm + k], k, s).start()
            return carry

        lax.fori_loop(0, tm, body, 0, unroll=8)

    @pl.when(t == 0)
    def _():
        issue(0, 0)

    @pl.when(t + 1 < nv)
    def _():
        issue(t + 1, 1 - slot)

    @pl.when(t < nv)
    def _():
        def body(k, carry):
            row_copy(0, k, slot).wait()
            return carry

        lax.fori_loop(0, tm, body, 0, unroll=8)
        o_ref[...] = buf_ref[slot].astype(BF16)


def _moe_gather(row_token, n_valid, x1, tm):
    d = x1.shape[1]
    r = row_token.shape[0]
    grid_spec = pltpu.PrefetchScalarGridSpec(
        num_scalar_prefetch=2,
        grid=(r // tm,),
        in_specs=[pl.BlockSpec(memory_space=pl.ANY)],
        out_specs=pl.BlockSpec((tm, d), lambda t, tok, nv: (jnp.minimum(t, nv[0] - 1), 0)),
        scratch_shapes=[pltpu.VMEM((2, tm, d), F32), pltpu.SemaphoreType.DMA((2,))],
    )
    need = 2 * tm * d * 4 + 2 * tm * d * 2 + 2 * tm * d * 4
    return pl.pallas_call(
        functools.partial(_moe_gather_kernel, tm=tm),
        grid_spec=grid_spec,
        out_shape=jax.ShapeDtypeStruct((r, d), BF16),
        compiler_params=_params(("arbitrary",), need + (4 << 20)),
        name="moe_gather",
    )(row_token, n_valid, x1)


def _moe_in_kernel(te_ref, nv_ref, first_ref, run_ref, nexte_ref, lastrun_ref, nruns_ref,
                   x_ref, w_hbm, a_ref, wbuf_ref, wgb_ref, wvb_ref, sem_ref,
                   *, layer, tf, nf, chunk):
    j = pl.program_id(0)
    t = pl.program_id(1)
    valid = t < nv_ref[0]
    c = j * nruns_ref[0] + run_ref[t]
    slot = c % 2
    starts_run = jnp.logical_and(valid, first_ref[t] == 1)

    def fetch(e, jj, s):
        col_g = pl.multiple_of(jj * tf, tf)
        col_v = pl.multiple_of((nf + jj) * tf, tf)
        return (pltpu.make_async_copy(w_hbm.at[layer, e, :, pl.ds(col_g, tf)], wbuf_ref.at[s, 0],
                                      sem_ref.at[s, 0]),
                pltpu.make_async_copy(w_hbm.at[layer, e, :, pl.ds(col_v, tf)], wbuf_ref.at[s, 1],
                                      sem_ref.at[s, 1]))

    @pl.when(jnp.logical_and(starts_run, c == 0))
    def _():
        for cp in fetch(te_ref[t], j, 0):
            cp.start()

    @pl.when(starts_run)
    def _():
        for cp in fetch(te_ref[t], j, slot):
            cp.wait()
        _cast_rows(wbuf_ref.at[slot, 0], wgb_ref, chunk)
        _cast_rows(wbuf_ref.at[slot, 1], wvb_ref, chunk)
        last_run = lastrun_ref[t] == 1

        @pl.when(jnp.logical_not(last_run))
        def _():
            for cp in fetch(nexte_ref[t], j, 1 - slot):
                cp.start()

        @pl.when(jnp.logical_and(last_run, j + 1 < nf))
        def _():
            for cp in fetch(te_ref[0], j + 1, 1 - slot):
                cp.start()

    @pl.when(valid)
    def _():
        x = x_ref[...]
        hg = jnp.dot(x, wgb_ref[...], preferred_element_type=F32)
        hv = jnp.dot(x, wvb_ref[...], preferred_element_type=F32)
        a_ref[...] = (hg * _sigmoid(hg) * hv).astype(a_ref.dtype)


def _moe_in(sched, xs, w_e_in, layer, de, tm):
    r, d = xs.shape
    tf = _pick(de, (512, 256, 128))
    nf = de // tf
    n_tiles = r // tm
    need = 2 * 2 * d * tf * 4 + 2 * d * tf * 2 + 2 * tm * d * 2 + 6 * tm * tf * 4

    def row(j, t, te, nv, *_):
        return (jnp.minimum(t, nv[0] - 1), 0)

    def out_map(j, t, te, nv, *_):
        return (jnp.minimum(t, nv[0] - 1), j)

    grid_spec = pltpu.PrefetchScalarGridSpec(
        num_scalar_prefetch=len(sched),
        grid=(nf, n_tiles),
        in_specs=[pl.BlockSpec((tm, d), row), pl.BlockSpec(memory_space=pl.ANY)],
        out_specs=pl.BlockSpec((tm, tf), out_map),
        scratch_shapes=[pltpu.VMEM((2, 2, d, tf), F32), pltpu.VMEM((d, tf), BF16),
                        pltpu.VMEM((d, tf), BF16), pltpu.SemaphoreType.DMA((2, 2))],
    )
    return pl.pallas_call(
        functools.partial(_moe_in_kernel, layer=layer, tf=tf, nf=nf, chunk=_pick(d, (256, 128, 8))),
        grid_spec=grid_spec,
        out_shape=jax.ShapeDtypeStruct((r, de), BF16),
        compiler_params=_params(("arbitrary", "arbitrary"), need + (4 << 20)),
        name="moe_in",
    )(*sched, xs, w_e_in)


def _moe_down_kernel(te_ref, nv_ref, first_ref, run_ref, nexte_ref, lastrun_ref, nruns_ref, dest_ref,
                     a_ref, w_hbm, out_hbm, wbuf_ref, wb_ref, y_ref, wsem_ref, sem_ref,
                     *, layer, chunk, tm):
    del nruns_ref
    t = pl.program_id(0)
    last = pl.num_programs(0) - 1
    nv = nv_ref[0]
    slot = t % 2
    wslot = run_ref[t] % 2
    starts_run = jnp.logical_and(t < nv, first_ref[t] == 1)

    def fetch(e, s):
        return pltpu.make_async_copy(w_hbm.at[layer, e], wbuf_ref.at[s], wsem_ref.at[s])

    def row_copy(k, s, dest):
        return pltpu.make_async_copy(y_ref.at[s, pl.ds(k, 1), :], out_hbm.at[pl.ds(dest, 1), :],
                                     sem_ref.at[s])

    def drain(s):
        def body(k, carry):
            row_copy(k, s, 0).wait()
            return carry

        lax.fori_loop(0, tm, body, 0, unroll=8)

    @pl.when(jnp.logical_and(t >= 2, t - 2 < nv))
    def _():
        drain(slot)

    @pl.when(jnp.logical_and(starts_run, t == 0))
    def _():
        fetch(te_ref[t], 0).start()

    @pl.when(starts_run)
    def _():
        fetch(te_ref[t], wslot).wait()
        _cast_rows(wbuf_ref.at[wslot], wb_ref, chunk)

        @pl.when(lastrun_ref[t] == 0)
        def _():
            fetch(nexte_ref[t], 1 - wslot).start()

    @pl.when(t < nv)
    def _():
        y_ref[slot] = jnp.dot(a_ref[...], wb_ref[...], preferred_element_type=F32)

        def body(k, carry):
            row_copy(k, slot, dest_ref[t * tm + k]).start()
            return carry

        lax.fori_loop(0, tm, body, 0, unroll=8)

    @pl.when(jnp.logical_and(t == last, jnp.logical_and(t >= 1, t - 1 < nv)))
    def _():
        drain(1 - slot)

    @pl.when(jnp.logical_and(t == last, t < nv))
    def _():
        drain(slot)


def _moe_down(sched, row_dest, a, w_e_down, layer, n_out_rows, tm):
    r, de = a.shape
    d = w_e_down.shape[3]
    n_tiles = r // tm
    need = 2 * de * d * 4 + de * d * 2 + 2 * tm * de * 2 + 3 * tm * d * 4

    def row(t, te, nv, *_):
        return (jnp.minimum(t, nv[0] - 1), 0)

    grid_spec = pltpu.PrefetchScalarGridSpec(
        num_scalar_prefetch=len(sched) + 1,
        grid=(n_tiles,),
        in_specs=[pl.BlockSpec((tm, de), row), pl.BlockSpec(memory_space=pl.ANY)],
        out_specs=pl.BlockSpec(memory_space=pl.ANY),
        scratch_shapes=[pltpu.VMEM((2, de, d), F32), pltpu.VMEM((de, d), BF16),
                        pltpu.VMEM((2, tm, d), F32),
                        pltpu.SemaphoreType.DMA((2,)), pltpu.SemaphoreType.DMA((2,))],
    )
    return pl.pallas_call(
        functools.partial(_moe_down_kernel, layer=layer, chunk=_pick(de, (256, 128, 8)), tm=tm),
        grid_spec=grid_spec,
        out_shape=jax.ShapeDtypeStruct((n_out_rows, d), F32),
        compiler_params=_params(("arbitrary",), need + (4 << 20)),
        name="moe_down",
    )(*sched, row_dest, a, w_e_down)


def _combine_ln_kernel(x_ref, y0_ref, y1_ref, r_ref, g_ref, b_ref, *outs, alpha, n_main, final):
    r = r_ref[...]
    y = alpha * x_ref[...] + r[:, 2:3] * y0_ref[...] + r[:, 3:4] * y1_ref[...]
    x2 = _layer_norm_rows(y, g_ref[...], b_ref[...])
    if final:
        main_ref, tail_ref = outs
        i = pl.program_id(0)

        @pl.when(i < n_main)
        def _():
            main_ref[...] = x2

        @pl.when(i >= n_main)
        def _():
            tail_ref[...] = x2
    else:
        o_ref, ob_ref = outs
        o_ref[...] = x2
        ob_ref[...] = x2.astype(BF16)


def _combine_ln(x1, y2, route, g, b, layer, alpha, n_p, final):
    n, d = x1.shape
    tm = LN_ROW_TILE
    steps = n // tm
    n_main = n_p // tm
    need = 2 * 4 * tm * d * 4 + 2 * tm * d * 2 + 4 * tm * d * 4
    blk = pl.BlockSpec((tm, d), lambda i: (i, 0))
    vec = pl.BlockSpec((None, 1, d), lambda i: (layer, 0, 0))
    if final:
        out_specs = [pl.BlockSpec((tm, d), lambda i: (jnp.minimum(i, n_main - 1), 0)),
                     pl.BlockSpec((tm, d), lambda i: (jnp.maximum(i - n_main, 0), 0))]
        out_shape = [jax.ShapeDtypeStruct((n_p, d), F32), jax.ShapeDtypeStruct((n - n_p, d), F32)]
    else:
        out_specs = [blk, blk]
        out_shape = [jax.ShapeDtypeStruct((n, d), F32), jax.ShapeDtypeStruct((n, d), BF16)]
    return pl.pallas_call(
        functools.partial(_combine_ln_kernel, alpha=alpha, n_main=n_main, final=final),
        grid=(steps,),
        in_specs=[blk, blk, pl.BlockSpec((tm, d), lambda i: (steps + i, 0)),
                  pl.BlockSpec((tm, LANES), lambda i: (i, 0)), vec, vec],
        out_specs=out_specs,
        out_shape=out_shape,
        compiler_params=_params(("arbitrary",), need + (4 << 20)),
        name="combine_ln",
    )(x1, y2, y2, route, g, b)


def _expert_rows(route, n_experts, tm):
    n_tok = route.shape[0]
    e = route[:, 0:2].astype(jnp.int32).reshape(-1)
    n_pairs = e.shape[0]
    onehot = (e[:, None] == jnp.arange(n_experts, dtype=jnp.int32)[None, :]).astype(jnp.int32)
    running = jnp.cumsum(onehot, axis=0)
    counts = running[-1]
    rank = jnp.sum(running * onehot, axis=1) - 1
    padded = (counts + tm - 1) // tm * tm
    ends = jnp.cumsum(padded)
    starts = ends - padded
    pos = jnp.sum(onehot * starts[None, :], axis=1) + rank
    n_rows = (n_pairs + n_experts * (tm - 1)) // tm * tm
    n_tiles = n_rows // tm
    row_pair = jnp.full((n_rows,), -1, jnp.int32).at[pos].set(jnp.arange(n_pairs, dtype=jnp.int32))
    rows = jnp.arange(n_rows, dtype=jnp.int32)
    real = row_pair >= 0
    row_token = jnp.where(real, row_pair // 2, 0)
    spare = 2 * n_tok + (rows // tm % 2) * tm + rows % tm
    row_dest = jnp.where(real, (row_pair % 2) * n_tok + row_pair // 2, spare)
    n_valid = (ends[-1] // tm).astype(jnp.int32)
    tile_start = jnp.arange(n_tiles, dtype=jnp.int32) * tm
    tile_expert = jnp.sum((tile_start[:, None] >= ends[None, :]).astype(jnp.int32), axis=1)
    last_expert = jnp.max(jnp.where(counts > 0, jnp.arange(n_experts, dtype=jnp.int32), 0))
    tile_expert = jnp.minimum(tile_expert, last_expert).astype(jnp.int32)
    tiles = jnp.arange(n_tiles, dtype=jnp.int32)
    prev_expert = jnp.concatenate([jnp.full((1,), -1, jnp.int32), tile_expert[:-1]])
    first = jnp.logical_and(tiles < n_valid, tile_expert != prev_expert)
    run = jnp.cumsum(first.astype(jnp.int32)) - 1
    start_at_or_after = lax.cummin(jnp.where(first, tiles, n_tiles), axis=0, reverse=True)
    next_start = jnp.concatenate([start_at_or_after[1:], jnp.full((1,), n_tiles, jnp.int32)])
    last_run = next_start >= n_tiles
    next_expert = jnp.where(last_run, tile_expert[0],
                            tile_expert[jnp.minimum(next_start, n_tiles - 1)])
    n_runs = jnp.sum(first.astype(jnp.int32)).reshape(1)
    sched = (tile_expert, n_valid.reshape(1), first.astype(jnp.int32), run.astype(jnp.int32),
             next_expert.astype(jnp.int32), last_run.astype(jnp.int32), n_runs)
    return row_token, row_dest, sched


def kernel(x_prompt, x_sample, cache_k, cache_v, cache_logf, state_conv, state_mlstm_c,
           state_mlstm_n, state_mlstm_m, page_table, w_in, b_in, conv_w, mlstm_norm_w,
           w_branch, w_out, ln1_g, ln1_b, w_router_group, b_router_group, w_router_expert,
           b_router_expert, w_expert_in, w_expert_down, ln2_g, ln2_b):
    bp, t, d = x_prompt.shape
    db = x_sample.shape[0]
    depth = w_in.shape[0]
    n_pool, ps, heads, hd = cache_k.shape[1:]
    aw = heads * hd
    cw = state_conv.shape[-1]
    mh, md = state_mlstm_c.shape[2], state_mlstm_c.shape[3]
    mw = mh * md
    groups = w_router_group.shape[-1]
    n_experts = w_router_expert.shape[-1]
    epg = n_experts // groups
    de = w_expert_down.shape[2]
    alpha = float((2 * depth) ** 0.25)
    n_p = bp * t
    n_tok = n_p + db
    n_all = n_p + ROW_TILE
    moe_tm = MOE_ROW_TILE

    c_f = 3 * aw
    c_i = c_f + heads + 3 * cw + 4 * mw
    g2_cols = 3 * cw + 4 * mw
    g3_cols = 3 * d
    zc_conv = 0
    zc_m = 3 * cw
    zc_g = g2_cols
    i0, f0 = heads, heads + mh
    assert heads + 2 * mh <= LANES and groups + n_experts <= LANES

    x_main = x_prompt.reshape(n_p, d)
    x_tail = jnp.concatenate([x_sample.reshape(db, d), jnp.zeros((ROW_TILE - db, d), F32)], axis=0)
    xb_all = jnp.concatenate([x_main, x_tail], axis=0).astype(BF16)
    b_in3 = b_in.reshape(depth, 1, b_in.shape[1])
    wt_in = jnp.swapaxes(w_in, 1, 2)

    cache_k2 = cache_k.reshape(depth, n_pool, ps * heads, hd)
    cache_v2 = cache_v.reshape(depth, n_pool, ps * heads, hd)
    cache_lft = jnp.swapaxes(cache_logf, 2, 3)
    gain3 = mlstm_norm_w.reshape(depth, 1, mw)
    ln1_g3, ln1_b3 = ln1_g.reshape(depth, 1, d), ln1_b.reshape(depth, 1, d)
    ln2_g3, ln2_b3 = ln2_g.reshape(depth, 1, d), ln2_b.reshape(depth, 1, d)

    def tail(rows):
        rows = rows.astype(BF16)
        return jnp.concatenate([rows, jnp.zeros((ROW_TILE - db, rows.shape[1]), BF16)], axis=0)

    outs = [[] for _ in range(12)]
    k_all = v_all = None
    for l in range(depth):
        b_l = b_in[l]
        b_main = jnp.concatenate([b_l[c_f + heads:c_i], b_l[c_i + 2 * mh:]]).reshape(1, -1)
        b_gate = jnp.concatenate([b_l[c_f:c_f + heads], b_l[c_i:c_i + 2 * mh],
                                  jnp.zeros((LANES - heads - 2 * mh,), F32)]).reshape(1, LANES)
        qb, kvb, k_all, v_all, kv_tail = _qkv_proj(xb_all, wt_in, b_in3, k_all, v_all, l, aw, n_p)
        zc = _main_proj(xb_all, wt_in, b_main, l, c_f, g2_cols, g3_cols, heads, heads + 2 * mh)
        zg = _gate_proj(xb_all, wt_in, b_gate, l, c_f, c_i, heads, mh)

        lf_p, cum, cumt = _prompt_gates(zg, bp, t)
        ya = _prompt_attention(qb, kvb, cum, cumt, n_all, bp, t, heads, hd)
        yc, conv_p = _prompt_conv(zc, conv_w, l, zc_conv, n_all, bp, t, cw)
        ym, c_p, n_p_state, m_p = _prompt_mlstm(zc, zg, gain3, l, zc_m, n_all, bp, t, mh, md, i0, f0)

        zs = zc[n_p:n_tok]
        zgs = zg[n_p:n_tok]
        yc_s, conv_s, ym_s, c_s, n_s, m_s, lf_s = _sample_mix(
            zs, zgs, conv_w, gain3, state_conv, state_mlstm_c, state_mlstm_n, state_mlstm_m,
            l, (zc_conv, zc_m), mh, md, i0, f0)
        q_s = qb[n_p:n_tok].astype(F32).reshape(db, heads, hd)
        k_s = kv_tail[:db, 0:aw].reshape(db, heads, hd)
        v_s = kv_tail[:db, aw:2 * aw].reshape(db, heads, hd)
        lf_new = lf_s[:, 0, 0:heads]
        ya_s = _decode_attention(page_table, q_s, k_s, v_s, lf_new.reshape(db, heads, 1),
                                 cache_k2, cache_v2, cache_lft, l)
        ya = _put_tail_rows(ya, tail(ya_s.reshape(db, aw)))
        yc = _put_tail_rows(yc, tail(yc_s.reshape(db, cw)))
        ym = _put_tail_rows(ym, tail(ym_s.reshape(db, mw)))

        merged = _merge(ya, yc, ym, zc, zc_g, w_branch, l, d)
        y1 = _outproj(merged, x_main, x_tail, w_out, l, alpha)

        wr = jnp.concatenate([w_router_group[l], w_router_expert[l],
                              jnp.zeros((d, LANES - groups - n_experts), F32)], axis=1)
        br = jnp.concatenate([b_router_group[l], b_router_expert[l],
                              jnp.zeros((LANES - groups - n_experts,), F32)]).reshape(1, LANES)
        x1, route = _ln_router(y1, ln1_g3, ln1_b3, wr, br, l, groups, epg)

        row_token, row_dest, sched = _expert_rows(route, n_experts, moe_tm)
        xs = _moe_gather(row_token, sched[1], x1, moe_tm)
        act = _moe_in(sched, xs, w_expert_in, l, de, moe_tm)
        y2 = _moe_down(sched, row_dest, act, w_expert_down, l, 2 * n_all + 2 * moe_tm, moe_tm)
        final = l == depth - 1
        x_main, x_next = _combine_ln(x1, y2, route, ln2_g3, ln2_b3, l, alpha, n_p, final)
        if final:
            y_prompt = x_main.reshape(bp, t, d)
            y_sample = x_next[:db].reshape(db, 1, d)
        else:
            x_tail, xb_all = x_main, x_next

        per_layer = (
            lf_p[:, 0:heads].reshape(bp, t, heads),
            k_s.reshape(db, 1, heads, hd),
            v_s.reshape(db, 1, heads, hd),
            lf_new.reshape(db, 1, heads),
            conv_p, conv_s,
            c_p, c_s,
            n_p_state.reshape(bp, mh, md), n_s,
            m_p.reshape(bp, mh), m_s.reshape(db, mh),
        )
        for acc, val in zip(outs, per_layer):
            acc.append(val)

    stacked = [jnp.stack(a, axis=0) for a in outs]
    return (y_prompt, y_sample,
            k_all.reshape(depth, bp, t, heads, hd), v_all.reshape(depth, bp, t, heads, hd),
            *stacked)
```

```python
import functools
import math

import jax
import jax.numpy as jnp
from jax import lax
from jax.experimental import pallas as pl
from jax.experimental.pallas import tpu as pltpu

F32 = jnp.float32
BF16 = jnp.bfloat16
HIGHEST = lax.Precision.HIGHEST
NEG_INF = float("-inf")

LANES = 128
ROW_TILE = 512
LN_ROW_TILE = 256
MOE_ROW_TILE = 256
V7X_VMEM_BYTES = 64 * 1024 * 1024
VMEM_RESERVE_BYTES = 6 * 1024 * 1024
LN_EPS = 1e-5
NT_DIMS = (((1,), (1,)), ((), ()))


def _vmem_limit(need_bytes):
    return int(min(V7X_VMEM_BYTES - VMEM_RESERVE_BYTES, max(need_bytes, 16 * 1024 * 1024)))


def _params(semantics, need_bytes):
    return pltpu.CompilerParams(dimension_semantics=semantics,
                                vmem_limit_bytes=_vmem_limit(need_bytes))


def _pick(n, prefs):
    for p in prefs:
        if n % p == 0:
            return p
    raise ValueError(f"no tile in {prefs} divides {n}")


def _log_sigmoid(x):
    return jnp.minimum(x, 0.0) - jnp.log1p(jnp.exp(-jnp.abs(x)))


def _sigmoid(x):
    return 1.0 / (1.0 + jnp.exp(-x))


def _cast_rows(src_ref, dst_ref, chunk):
    rows = src_ref.shape[0]

    def body(c, carry):
        r = pl.multiple_of(c * chunk, chunk)
        dst_ref[pl.ds(r, chunk), :] = src_ref[pl.ds(r, chunk), :].astype(BF16)
        return carry

    lax.fori_loop(0, rows // chunk, body, 0)


def _qkv_proj_kernel(*refs, chunk, nq, n_main, n_rows, aliased):
    x_ref, w_ref, b_ref = refs[:3]
    outs = refs[3 + (2 if aliased else 0):]
    qb_ref, kvb_ref, k_ref, v_ref, tail_ref, wb_ref = outs
    j = pl.program_id(0)
    i = pl.program_id(1)

    @pl.when(i == 0)
    def _():
        _cast_rows(w_ref, wb_ref, chunk)

    z = lax.dot_general(x_ref[...], wb_ref[...], NT_DIMS, preferred_element_type=F32) + b_ref[...]

    @pl.when(j < nq)
    def _():
        qb_ref[...] = z.astype(BF16)

    @pl.when(j >= nq)
    def _():
        kvb_ref[...] = z.astype(BF16)

    @pl.when(jnp.logical_and(jnp.logical_and(j >= nq, j < 2 * nq), i < n_main))
    def _():
        k_ref[...] = z

    @pl.when(jnp.logical_and(j >= 2 * nq, i < n_main))
    def _():
        v_ref[...] = z

    @pl.when(jnp.logical_and(j >= nq, i == n_rows - 1))
    def _():
        tail_ref[...] = z


def _qkv_proj(xb, wt_in, b_in3, k_all, v_all, layer, aw, n_p):
    n, k = xb.shape
    depth = wt_in.shape[0]
    tm = ROW_TILE
    tn = _pick(aw, (1024, 512, 256, 128))
    nq = aw // tn
    n_rows = n // tm
    n_main = n_p // tm
    aliased = k_all is not None

    def k_map(j, i):
        row = jnp.where(j < nq, 0, jnp.where(j < 2 * nq, jnp.minimum(i, n_main - 1), n_main - 1))
        return (layer, row, jnp.clip(j - nq, 0, nq - 1))

    def v_map(j, i):
        row = jnp.where(j < 2 * nq, 0, jnp.minimum(i, n_main - 1))
        return (layer, row, jnp.clip(j - 2 * nq, 0, nq - 1))

    def q_map(j, i):
        return (jnp.where(j < nq, i, n_rows - 1), jnp.minimum(j, nq - 1))

    def kvb_map(j, i):
        return (jnp.where(j < nq, 0, i), jnp.maximum(j - nq, 0))

    in_specs = [pl.BlockSpec((tm, k), lambda j, i: (i, 0)),
                pl.BlockSpec((None, tn, k), lambda j, i: (layer, j, 0),
                             pipeline_mode=pl.Buffered(1)),
                pl.BlockSpec((None, 1, tn), lambda j, i: (layer, 0, j))]
    args = [xb, wt_in, b_in3]
    aliases = {}
    if aliased:
        in_specs += [pl.BlockSpec(memory_space=pl.ANY), pl.BlockSpec(memory_space=pl.ANY)]
        args += [k_all, v_all]
        aliases = {3: 2, 4: 3}
    need = k * tn * 4 + k * tn * 2 + 2 * tm * k * 2 + 2 * tm * tn * (2 + 2 + 4 + 4 + 4) + 2 * tm * tn * 4
    return pl.pallas_call(
        functools.partial(_qkv_proj_kernel, chunk=_pick(tn, (64, 16)), nq=nq, n_main=n_main,
                          n_rows=n_rows, aliased=aliased),
        grid=(3 * nq, n_rows),
        in_specs=in_specs,
        out_specs=[pl.BlockSpec((tm, tn), q_map),
                   pl.BlockSpec((tm, tn), kvb_map),
                   pl.BlockSpec((None, tm, tn), k_map),
                   pl.BlockSpec((None, tm, tn), v_map),
                   pl.BlockSpec((tm, tn), lambda j, i: (0, jnp.maximum(j - nq, 0)))],
        out_shape=[jax.ShapeDtypeStruct((n, aw), BF16),
                   jax.ShapeDtypeStruct((n, 2 * aw), BF16),
                   jax.ShapeDtypeStruct((depth, n_p, aw), F32),
                   jax.ShapeDtypeStruct((depth, n_p, aw), F32),
                   jax.ShapeDtypeStruct((tm, 2 * aw), F32)],
        scratch_shapes=[pltpu.VMEM((tn, k), BF16)],
        input_output_aliases=aliases,
        compiler_params=_params(("arbitrary", "arbitrary"), need + (4 << 20)),
        name="qkv_proj",
    )(*args)


def _main_proj_kernel(x_ref, w_ref, wn_ref, b_ref, o_ref, wb_ref,
                      *, chunk, g2_tiles, shift2, shift3):
    j = pl.program_id(0)
    i = pl.program_id(1)
    tn = w_ref.shape[0]

    def prepare(shift):
        def body(c, carry):
            r = pl.multiple_of(c * chunk, chunk)
            wb_ref[pl.ds(r, chunk), :] = w_ref[pl.ds(r + shift, chunk), :].astype(BF16)
            return carry

        lax.fori_loop(0, (tn - shift) // chunk, body, 0)
        wb_ref[tn - shift:tn, :] = wn_ref[0:shift, :].astype(BF16)

    @pl.when(jnp.logical_and(i == 0, j < g2_tiles))
    def _():
        prepare(shift2)

    @pl.when(jnp.logical_and(i == 0, j >= g2_tiles))
    def _():
        prepare(shift3)

    acc = lax.dot_general(x_ref[...], wb_ref[...], NT_DIMS, preferred_element_type=F32)
    o_ref[...] = acc + b_ref[...]


def _main_proj(xb, wt_in, b_main, layer, col0, g2_cols, g3_cols, shift2, shift3):
    n, k = xb.shape
    tm = ROW_TILE
    tn = math.gcd(math.gcd(col0, g2_cols), math.gcd(g3_cols, 1024))
    base = col0 // tn
    nb = -(-shift3 // 8) * 8
    assert shift2 <= shift3 and tn % nb == 0
    chunk = math.gcd(math.gcd(shift2, shift3), 64)
    g2_tiles = g2_cols // tn
    n_tiles = (g2_cols + g3_cols) // tn
    need = 2 * k * tn * 4 + 2 * k * nb * 4 + k * tn * 2 + 2 * tm * k * 2 + 2 * tm * tn * 4
    return pl.pallas_call(
        functools.partial(_main_proj_kernel, chunk=chunk, g2_tiles=g2_tiles,
                          shift2=shift2, shift3=shift3),
        grid=(n_tiles, n // tm),
        in_specs=[pl.BlockSpec((tm, k), lambda j, i: (i, 0)),
                  pl.BlockSpec((None, tn, k), lambda j, i: (layer, base + j, 0)),
                  pl.BlockSpec((None, nb, k), lambda j, i: (layer, (base + j + 1) * (tn // nb), 0)),
                  pl.BlockSpec((1, tn), lambda j, i: (0, j))],
        out_specs=pl.BlockSpec((tm, tn), lambda j, i: (i, j)),
        out_shape=jax.ShapeDtypeStruct((n, g2_cols + g3_cols), F32),
        scratch_shapes=[pltpu.VMEM((tn, k), BF16)],
        compiler_params=_params(("arbitrary", "arbitrary"), need + (4 << 20)),
        name="main_proj",
    )(xb, wt_in, wt_in, b_main)


def _gate_proj_kernel(x_ref, wf_ref, wi_ref, b_ref, o_ref, wb_ref):
    nf = wf_ref.shape[0]
    ni = wi_ref.shape[0]

    @pl.when(pl.program_id(0) == 0)
    def _():
        wb_ref[...] = jnp.zeros_like(wb_ref)
        wb_ref[0:nf, :] = wf_ref[...].astype(BF16)
        wb_ref[nf:nf + ni, :] = wi_ref[...].astype(BF16)

    acc = lax.dot_general(x_ref[...], wb_ref[...], NT_DIMS, preferred_element_type=F32)
    o_ref[...] = acc + b_ref[...]


def _gate_proj(xb, wt_in, b_gate, layer, c_f, c_i, heads, mh):
    n, k = xb.shape
    tm = ROW_TILE
    assert c_f % heads == 0 and c_i % (2 * mh) == 0
    need = 4 * (heads + 2 * mh) * k * 4 + LANES * k * 2 + 2 * tm * k * 2 + 3 * tm * LANES * 4
    return pl.pallas_call(
        _gate_proj_kernel,
        grid=(n // tm,),
        in_specs=[pl.BlockSpec((tm, k), lambda i: (i, 0)),
                  pl.BlockSpec((None, heads, k), lambda i: (layer, c_f // heads, 0)),
                  pl.BlockSpec((None, 2 * mh, k), lambda i: (layer, c_i // (2 * mh), 0)),
                  pl.BlockSpec((1, LANES), lambda i: (0, 0))],
        out_specs=pl.BlockSpec((tm, LANES), lambda i: (i, 0)),
        out_shape=jax.ShapeDtypeStruct((n, LANES), F32),
        scratch_shapes=[pltpu.VMEM((LANES, k), BF16)],
        compiler_params=_params(("arbitrary",), need + (4 << 20)),
        name="gate_proj",
    )(xb, wt_in, wt_in, b_gate)


def _merge_kernel(ya_ref, yc_ref, ym_ref, ga_ref, gc_ref, gm_ref, w_ref, o_ref, wb_ref,
                  *, chunk, aw, cw):
    @pl.when(pl.program_id(1) == 0)
    def _():
        _cast_rows(w_ref, wb_ref, chunk)

    da = jnp.dot(ya_ref[...], wb_ref[0:aw, :], preferred_element_type=F32)
    dc = jnp.dot(yc_ref[...], wb_ref[aw:aw + cw, :], preferred_element_type=F32)
    dm = jnp.dot(ym_ref[...], wb_ref[aw + cw:, :], preferred_element_type=F32)
    out = _sigmoid(ga_ref[...]) * da + _sigmoid(gc_ref[...]) * dc + _sigmoid(gm_ref[...]) * dm
    o_ref[...] = out.astype(o_ref.dtype)


def _merge(ya, yc, ym, zc, g_col0, w_branch_l, layer, d):
    n = ya.shape[0]
    aw, cw, mw = ya.shape[1], yc.shape[1], ym.shape[1]
    mix = aw + cw + mw
    tm = ROW_TILE
    tn = _pick(d, (512, 256, 128))
    gb = g_col0 // tn
    nd = d // tn
    need = mix * tn * 4 + mix * tn * 2 + 2 * tm * mix * 2 + 10 * tm * tn * 4
    return pl.pallas_call(
        functools.partial(_merge_kernel, chunk=_pick(mix, (256, 128, 8)), aw=aw, cw=cw),
        grid=(nd, n // tm),
        in_specs=[pl.BlockSpec((tm, aw), lambda j, i: (i, 0)),
                  pl.BlockSpec((tm, cw), lambda j, i: (i, 0)),
                  pl.BlockSpec((tm, mw), lambda j, i: (i, 0)),
                  pl.BlockSpec((tm, tn), lambda j, i: (i, gb + j)),
                  pl.BlockSpec((tm, tn), lambda j, i: (i, gb + nd + j)),
                  pl.BlockSpec((tm, tn), lambda j, i: (i, gb + 2 * nd + j)),
                  pl.BlockSpec((None, mix, tn), lambda j, i: (layer, 0, j),
                               pipeline_mode=pl.Buffered(1))],
        out_specs=pl.BlockSpec((tm, tn), lambda j, i: (i, j)),
        out_shape=jax.ShapeDtypeStruct((n, d), BF16),
        scratch_shapes=[pltpu.VMEM((mix, tn), BF16)],
        compiler_params=_params(("arbitrary", "arbitrary"), need + (4 << 20)),
        name="merge",
    )(ya, yc, ym, zc, zc, zc, w_branch_l)


def _outproj_kernel(m_ref, x_ref, xt_ref, w_ref, o_ref, wb_ref, *, chunk, alpha, n_main):
    i = pl.program_id(1)

    @pl.when(i == 0)
    def _():
        _cast_rows(w_ref, wb_ref, chunk)

    acc = jnp.dot(m_ref[...], wb_ref[...], preferred_element_type=F32)

    @pl.when(i < n_main)
    def _():
        o_ref[...] = alpha * x_ref[...] + acc

    @pl.when(i >= n_main)
    def _():
        o_ref[...] = alpha * xt_ref[...] + acc


def _outproj(mb, x_main, x_tail, w_out, layer, alpha):
    n, k = mb.shape
    d = w_out.shape[2]
    tm = ROW_TILE
    tn = _pick(d, (512, 256, 128))
    n_main = min(x_main.shape[0] // tm, n // tm)
    tail_row = x_tail.shape[0] // tm - 1
    need = 2 * k * tn * 4 + k * tn * 2 + 2 * tm * k * 2 + 8 * tm * tn * 4
    return pl.pallas_call(
        functools.partial(_outproj_kernel, chunk=_pick(k, (256, 128, 8)), alpha=alpha, n_main=n_main),
        grid=(d // tn, n // tm),
        in_specs=[pl.BlockSpec((tm, k), lambda j, i: (i, 0)),
                  pl.BlockSpec((tm, tn), lambda j, i: (jnp.minimum(i, n_main - 1), j)),
                  pl.BlockSpec((tm, tn), lambda j, i: (tail_row, j)),
                  pl.BlockSpec((None, k, tn), lambda j, i: (layer, 0, j))],
        out_specs=pl.BlockSpec((tm, tn), lambda j, i: (i, j)),
        out_shape=jax.ShapeDtypeStruct((n, d), F32),
        scratch_shapes=[pltpu.VMEM((k, tn), BF16)],
        compiler_params=_params(("arbitrary", "arbitrary"), need + (4 << 20)),
        name="outproj",
    )(mb, x_main, x_tail, w_out)


def _gates_kernel(zg_ref, lf_ref, cum_ref, cumt_ref, carry_ref):
    tt = zg_ref.shape[0]

    @pl.when(pl.program_id(1) == 0)
    def _():
        carry_ref[...] = jnp.zeros_like(carry_ref)

    lf = _log_sigmoid(zg_ref[...])
    row = lax.broadcasted_iota(jnp.int32, (tt, tt), 0)
    col = lax.broadcasted_iota(jnp.int32, (tt, tt), 1)
    tri = (row >= col).astype(F32)
    cum = jnp.dot(tri, lf, precision=HIGHEST, preferred_element_type=F32) + carry_ref[...]
    carry_ref[...] = cum[tt - 1:tt, :]
    lf_ref[...] = lf
    cum_ref[...] = cum
    cumt_ref[0] = cum.T


def _prompt_gates(zg, b, t):
    tt = _pick(t, (256, 128))
    nt = t // tt
    return pl.pallas_call(
        _gates_kernel,
        grid=(b, nt),
        in_specs=[pl.BlockSpec((tt, LANES), lambda bi, ti: (bi * nt + ti, 0))],
        out_specs=[pl.BlockSpec((tt, LANES), lambda bi, ti: (bi * nt + ti, 0)),
                   pl.BlockSpec((tt, LANES), lambda bi, ti: (bi * nt + ti, 0)),
                   pl.BlockSpec((1, LANES, tt), lambda bi, ti: (bi, 0, ti))],
        out_shape=[jax.ShapeDtypeStruct((b * t, LANES), F32),
                   jax.ShapeDtypeStruct((b * t, LANES), F32),
                   jax.ShapeDtypeStruct((b, LANES, t), F32)],
        scratch_shapes=[pltpu.VMEM((1, LANES), F32)],
        compiler_params=_params(("arbitrary", "arbitrary"), 0),
        name="prompt_gates",
    )(zg)


def _attn_kernel(q_ref, k_ref, v_ref, cq_ref, ck_ref, o_ref, m_ref, l_ref, acc_ref,
                 *, heads, hd, scale, qs):
    qi = pl.program_id(1)
    ki = pl.program_id(2)
    tq = q_ref.shape[0]
    tk = k_ref.shape[0]

    @pl.when(ki == 0)
    def _():
        m_ref[...] = jnp.full(m_ref.shape, -1e30, F32)
        l_ref[...] = jnp.zeros_like(l_ref)
        acc_ref[...] = jnp.zeros_like(acc_ref)

    def block(diagonal):
        cq = cq_ref[...]
        ck = ck_ref[0]
        for h in range(heads):
            sl = slice(h * hd, (h + 1) * hd)
            kh = k_ref[:, sl]
            vh = v_ref[:, sl]
            ckh = ck[h:h + 1, :]
            for r in range(tq // qs):
                rs = slice(r * qs, (r + 1) * qs)
                s = lax.dot_general(q_ref[rs, sl], kh, NT_DIMS, preferred_element_type=F32) * scale
                s = s + (cq[rs, h:h + 1] - ckh)
                if diagonal:
                    row = lax.broadcasted_iota(jnp.int32, (qs, tk), 0) + r * qs
                    col = lax.broadcasted_iota(jnp.int32, (qs, tk), 1)
                    s = jnp.where(col <= row, s, -1e30)
                m_old = m_ref[h, rs]
                m_new = jnp.maximum(m_old, jnp.max(s, axis=1, keepdims=True))
                p = jnp.exp(s - m_new)
                a = jnp.exp(m_old - m_new)
                l_ref[h, rs] = a * l_ref[h, rs] + jnp.sum(p, axis=1, keepdims=True)
                acc_ref[rs, sl] = a * acc_ref[rs, sl] + jnp.dot(p.astype(BF16), vh,
                                                                preferred_element_type=F32)
                m_ref[h, rs] = m_new

    @pl.when(ki < qi)
    def _():
        block(False)

    @pl.when(ki == qi)
    def _():
        block(True)
        for h in range(heads):
            sl = slice(h * hd, (h + 1) * hd)
            o_ref[:, sl] = (acc_ref[:, sl] / l_ref[h]).astype(o_ref.dtype)


def _prompt_attention(qb, kvb, cum, cumt, n_all, b, t, heads, hd):
    aw = heads * hd
    tq = _pick(t, (256, 128))
    nt = t // tq
    need = 2 * 3 * tq * aw * 2 + 2 * tq * aw * 2 + tq * aw * 4 + 2 * heads * tq * LANES * 4
    return pl.pallas_call(
        functools.partial(_attn_kernel, heads=heads, hd=hd, scale=hd ** -0.5, qs=tq),
        grid=(b, nt, nt),
        in_specs=[pl.BlockSpec((tq, aw), lambda bi, qi, ki: (bi * nt + qi, 0)),
                  pl.BlockSpec((tq, aw), lambda bi, qi, ki: (bi * nt + jnp.minimum(ki, qi), 0)),
                  pl.BlockSpec((tq, aw), lambda bi, qi, ki: (bi * nt + jnp.minimum(ki, qi), 1)),
                  pl.BlockSpec((tq, LANES), lambda bi, qi, ki: (bi * nt + qi, 0)),
                  pl.BlockSpec((1, LANES, tq), lambda bi, qi, ki: (bi, 0, jnp.minimum(ki, qi)))],
        out_specs=pl.BlockSpec((tq, aw), lambda bi, qi, ki: (bi * nt + qi, 0)),
        out_shape=jax.ShapeDtypeStruct((n_all, aw), BF16),
        scratch_shapes=[pltpu.VMEM((heads, tq, 1), F32),
                        pltpu.VMEM((heads, tq, 1), F32),
                        pltpu.VMEM((tq, aw), F32)],
        compiler_params=_params(("arbitrary", "arbitrary", "arbitrary"), need + (8 << 20)),
        name="prompt_attention",
    )(qb, kvb, kvb, cum, cumt)


def _decode_kernel(pt_ref, q_ref, kn_ref, vn_ref, lfn_ref, *rest, pps, heads, scale):
    del pt_ref
    k_refs = rest[:pps]
    v_refs = rest[pps:2 * pps]
    lf_refs = rest[2 * pps:3 * pps]
    o_ref = rest[3 * pps]
    m_ref, l_ref, acc_ref, carry_ref = rest[3 * pps + 1:]
    p_idx = pl.program_id(1)
    ps = lf_refs[0].shape[1]
    hd = q_ref.shape[2]
    q = q_ref[0]

    @pl.when(p_idx == 0)
    def _():
        m_ref[...] = jnp.sum(q * kn_ref[0], axis=1, keepdims=True) * scale
        l_ref[...] = jnp.ones_like(l_ref)
        acc_ref[...] = vn_ref[0]
        carry_ref[...] = lfn_ref[0]

    del hd
    qb = q.astype(BF16)
    row = lax.broadcasted_iota(jnp.int32, (ps, ps), 0)
    col = lax.broadcasted_iota(jnp.int32, (ps, ps), 1)
    later = (row > col).astype(F32)
    flat = lax.broadcasted_iota(jnp.int32, (heads, ps * heads), 1)
    own_head = lax.rem(flat, heads) == lax.broadcasted_iota(jnp.int32, (heads, ps * heads), 0)
    pos_of = lax.broadcasted_iota(jnp.int32, (ps, ps * heads), 1) // heads
    spread = (pos_of == lax.broadcasted_iota(jnp.int32, (ps, ps * heads), 0)).astype(BF16)
    carry = carry_ref[...]
    scores = []
    for i in range(pps):
        lf = lf_refs[i][...]
        bias = jnp.dot(lf, later, precision=HIGHEST, preferred_element_type=F32) + carry
        carry = carry + jnp.sum(lf, axis=1, keepdims=True)
        b0 = bias.astype(BF16)
        r1 = bias - b0.astype(F32)
        b1 = r1.astype(BF16)
        b2 = (r1 - b1.astype(F32)).astype(BF16)
        pieces = jnp.dot(jnp.concatenate([b0, b1, b2], axis=0), spread, preferred_element_type=F32)
        bias_flat = pieces[0:heads] + pieces[heads:2 * heads] + pieces[2 * heads:3 * heads]
        kb = k_refs[i][...].astype(BF16)
        s = lax.dot_general(qb, kb, NT_DIMS, preferred_element_type=F32) * scale + bias_flat
        scores.append(jnp.where(own_head, s, -1e30))
    carry_ref[...] = carry
    m_old = m_ref[...]
    m_new = m_old
    for s in scores:
        m_new = jnp.maximum(m_new, jnp.max(s, axis=1, keepdims=True))
    a = jnp.exp(m_old - m_new)
    l_new = a * l_ref[...]
    acc = a * acc_ref[...]
    for i in range(pps):
        pr = jnp.exp(scores[i] - m_new)
        l_new = l_new + jnp.sum(pr, axis=1, keepdims=True)
        acc = acc + jnp.dot(pr.astype(BF16), v_refs[i][...].astype(BF16),
                            preferred_element_type=F32)
    l_ref[...] = l_new
    acc_ref[...] = acc
    m_ref[...] = m_new

    @pl.when(p_idx == pl.num_programs(1) - 1)
    def _():
        o_ref[0] = acc_ref[...] / l_ref[...]


def _decode_attention(page_table, q, k_new, v_new, lf_new, cache_k2, cache_v2, cache_lft, layer):
    db, heads, hd = q.shape
    n_pages = page_table.shape[1]
    ps = cache_lft.shape[3]
    pps = _pick(n_pages, (8, 4, 2, 1))
    steps = n_pages // pps
    pt_flat = page_table.reshape(-1)

    def page_map(i):
        def index_map(b, p, pt):
            return (layer, pt[b * n_pages + (n_pages - 1 - (p * pps + i))], 0, 0)
        return index_map

    small = lambda b, p, pt: (b, 0, 0)
    in_specs = [pl.BlockSpec((1, heads, hd), small), pl.BlockSpec((1, heads, hd), small),
                pl.BlockSpec((1, heads, hd), small), pl.BlockSpec((1, heads, 1), small)]
    in_specs += [pl.BlockSpec((None, None, ps * heads, hd), page_map(i)) for i in range(pps)]
    in_specs += [pl.BlockSpec((None, None, ps * heads, hd), page_map(i)) for i in range(pps)]
    in_specs += [pl.BlockSpec((None, None, heads, ps), page_map(i)) for i in range(pps)]
    need = 2 * 2 * pps * ps * heads * hd * 4
    grid_spec = pltpu.PrefetchScalarGridSpec(
        num_scalar_prefetch=1,
        grid=(db, steps),
        in_specs=in_specs,
        out_specs=pl.BlockSpec((1, heads, hd), small),
        scratch_shapes=[pltpu.VMEM((heads, 1), F32), pltpu.VMEM((heads, 1), F32),
                        pltpu.VMEM((heads, hd), F32), pltpu.VMEM((heads, 1), F32)],
    )
    return pl.pallas_call(
        functools.partial(_decode_kernel, pps=pps, heads=heads, scale=hd ** -0.5),
        grid_spec=grid_spec,
        out_shape=jax.ShapeDtypeStruct((db, heads, hd), F32),
        compiler_params=_params(("arbitrary", "arbitrary"), need + (8 << 20)),
        name="decode_attention",
    )(pt_flat, q, k_new, v_new, lf_new,
      *([cache_k2] * pps), *([cache_v2] * pps), *([cache_lft] * pps))


def _conv_kernel(cb_ref, cc_ref, ch_ref, w_ref, y_ref, cn_ref):
    t = cc_ref.shape[0]
    u = cc_ref[...] * ch_ref[...]
    row = lax.broadcasted_iota(jnp.int32, u.shape, 0)
    u1 = jnp.where(row >= 1, pltpu.roll(u, 1, axis=0), 0.0)
    u2 = jnp.where(row >= 2, pltpu.roll(u, 2, axis=0), 0.0)
    w = w_ref[...]
    y = cb_ref[...] * (w[0:1, :] * u2 + w[1:2, :] * u1 + w[2:3, :] * u)
    y_ref[...] = y.astype(y_ref.dtype)
    cn_ref[0] = u[t - 2:t, :]


def _prompt_conv(zc, conv_w, layer, col0, n_all, b, t, cw):
    tc = _pick(cw, (256, 128))
    cb0 = col0 // tc
    nc = cw // tc
    kk = conv_w.shape[1]
    need = 2 * 3 * t * tc * 4 + 2 * t * tc * 2 + 6 * t * tc * 4
    return pl.pallas_call(
        _conv_kernel,
        grid=(b, nc),
        in_specs=[pl.BlockSpec((t, tc), lambda bi, j: (bi, cb0 + j)),
                  pl.BlockSpec((t, tc), lambda bi, j: (bi, cb0 + nc + j)),
                  pl.BlockSpec((t, tc), lambda bi, j: (bi, cb0 + 2 * nc + j)),
                  pl.BlockSpec((None, kk, tc), lambda bi, j: (layer, 0, j))],
        out_specs=[pl.BlockSpec((t, tc), lambda bi, j: (bi, j)),
                   pl.BlockSpec((1, kk - 1, tc), lambda bi, j: (bi, 0, j))],
        out_shape=[jax.ShapeDtypeStruct((n_all, cw), BF16),
                   jax.ShapeDtypeStruct((b, kk - 1, cw), F32)],
        compiler_params=_params(("arbitrary", "arbitrary"), need + (4 << 20)),
        name="prompt_conv",
    )(zc, zc, zc, conv_w)


def _head_norm(x, gain_row):
    mu = jnp.mean(x, axis=1, keepdims=True)
    xc = x - mu
    var = jnp.mean(xc * xc, axis=1, keepdims=True)
    return xc * lax.rsqrt(var + LN_EPS) * gain_row


def _mlstm_kernel(q_ref, k_ref, v_ref, o_ref, zg_ref, gain_ref, y_ref, c_out, n_out, m_out,
                  c_ref, n_ref, m_ref, *, heads, md, i0, f0):
    ci = pl.program_id(1)
    lc = q_ref.shape[0]

    @pl.when(ci == 0)
    def _():
        c_ref[...] = jnp.zeros_like(c_ref)
        n_ref[...] = jnp.zeros_like(n_ref)
        m_ref[...] = jnp.zeros_like(m_ref)

    zg = zg_ref[...]
    zgt = zg.T
    row = lax.broadcasted_iota(jnp.int32, (lc, lc), 0)
    col = lax.broadcasted_iota(jnp.int32, (lc, lc), 1)
    causal = col <= row
    bcols = jnp.dot(causal.astype(F32), _log_sigmoid(zg), precision=HIGHEST,
                    preferred_element_type=F32)
    brows = bcols.T
    kscale = md ** -0.5
    for h in range(heads):
        sl = slice(h * md, (h + 1) * md)
        bcol = bcols[:, f0 + h:f0 + h + 1]
        brow = brows[f0 + h:f0 + h + 1, :]
        icol = zg[:, i0 + h:i0 + h + 1]
        irow = zgt[i0 + h:i0 + h + 1, :]
        m_prev = m_ref[h]
        cst = c_ref[h]
        nst = n_ref[h]
        qh = q_ref[:, sl]
        kh = k_ref[:, sl] * kscale
        qb = qh.astype(BF16)
        kb = kh.astype(BF16)
        vb = v_ref[:, sl].astype(BF16)

        dmat = jnp.where(causal, bcol - brow + irow, NEG_INF)
        inter = bcol + m_prev
        m_t = jnp.maximum(jnp.max(dmat, axis=1, keepdims=True), inter)
        w_intra = jnp.exp(dmat - m_t)
        w_inter = jnp.exp(inter - m_t)
        s = lax.dot_general(qb, kb, NT_DIMS, preferred_element_type=F32) * w_intra
        num = jnp.dot(s.astype(BF16), vb, preferred_element_type=F32) + w_inter * lax.dot_general(
            qb, cst.astype(BF16), NT_DIMS, preferred_element_type=F32)
        den = jnp.sum(s, axis=1, keepdims=True) + w_inter * jnp.sum(qh * nst, axis=1, keepdims=True)
        hh = num / jnp.maximum(jnp.abs(den), jnp.exp(-m_t))
        og = _sigmoid(o_ref[:, sl])
        y_ref[:, sl] = _head_norm(og * hh, gain_ref[:, sl]).astype(y_ref.dtype)

        g = bcol[lc - 1:lc, :]
        acol = g - bcol + icol
        m_new = jnp.maximum(g + m_prev, jnp.max(acol, axis=0, keepdims=True))
        decay = jnp.exp(g + m_prev - m_new)
        wa = jnp.exp(acol - m_new)
        vwt = (v_ref[:, sl] * wa).T.astype(BF16)
        c_ref[h] = decay * cst + jnp.dot(vwt, kb, preferred_element_type=F32)
        n_ref[h] = decay * nst + jnp.sum(kh * wa, axis=0, keepdims=True)
        m_ref[h] = m_new

    @pl.when(ci == pl.num_programs(1) - 1)
    def _():
        c_out[0] = c_ref[...]
        n_out[0] = n_ref[...]
        m_out[0] = m_ref[...]


def _prompt_mlstm(zc, zg, gain, layer, col0, n_all, b, t, heads, md, i0, f0):
    mw = heads * md
    lc = _pick(t, (256, 128))
    nch = t // lc
    assert col0 % mw == 0
    qb0 = col0 // mw
    need = 2 * 4 * lc * mw * 4 + 2 * lc * mw * 2 + 3 * heads * md * md * 4 + 12 * lc * lc * 4
    zblock = lambda off: pl.BlockSpec((lc, mw), lambda bi, ci: (bi * nch + ci, qb0 + off))
    return pl.pallas_call(
        functools.partial(_mlstm_kernel, heads=heads, md=md, i0=i0, f0=f0),
        grid=(b, nch),
        in_specs=[zblock(0), zblock(1), zblock(2), zblock(3),
                  pl.BlockSpec((lc, LANES), lambda bi, ci: (bi * nch + ci, 0)),
                  pl.BlockSpec((None, 1, mw), lambda bi, ci: (layer, 0, 0))],
        out_specs=[pl.BlockSpec((lc, mw), lambda bi, ci: (bi * nch + ci, 0)),
                   pl.BlockSpec((1, heads, md, md), lambda bi, ci: (bi, 0, 0, 0)),
                   pl.BlockSpec((1, heads, 1, md), lambda bi, ci: (bi, 0, 0, 0)),
                   pl.BlockSpec((1, heads, 1, 1), lambda bi, ci: (bi, 0, 0, 0))],
        out_shape=[jax.ShapeDtypeStruct((n_all, mw), BF16),
                   jax.ShapeDtypeStruct((b, heads, md, md), F32),
                   jax.ShapeDtypeStruct((b, heads, 1, md), F32),
                   jax.ShapeDtypeStruct((b, heads, 1, 1), F32)],
        scratch_shapes=[pltpu.VMEM((heads, md, md), F32),
                        pltpu.VMEM((heads, 1, md), F32),
                        pltpu.VMEM((heads, 1, 1), F32)],
        compiler_params=_params(("arbitrary", "arbitrary"), need + (8 << 20)),
        name="prompt_mlstm",
    )(zc, zc, zc, zc, zg, gain)


def _sample_mix_kernel(cb_ref, cc_ref, ch_ref, cw_ref, prev_ref, q_ref, k_ref, v_ref, vcol_ref,
                       o_ref, zg_ref, gain_ref, c0_ref, n0_ref, m0_ref,
                       yc_ref, cn_ref, ym_ref, c1_ref, n1_ref, m1_ref, lf_ref,
                       *, heads, md, i0, f0):
    u = cc_ref[0] * ch_ref[0]
    prev = prev_ref[0]
    w = cw_ref[...]
    yc_ref[0] = cb_ref[0] * (w[0:1, :] * prev[0:1, :] + w[1:2, :] * prev[1:2, :] + w[2:3, :] * u)
    cn_ref[0, 0:1, :] = prev[1:2, :]
    cn_ref[0, 1:2, :] = u

    zg = zg_ref[0]
    lfz = _log_sigmoid(zg)
    lf_ref[0] = lfz
    lane = lax.broadcasted_iota(jnp.int32, (1, heads), 1)
    m_row = jnp.zeros((1, heads), F32)
    kscale = md ** -0.5
    for h in range(heads):
        sl = slice(h * md, (h + 1) * md)
        qh = q_ref[0][:, sl]
        kh = k_ref[0][:, sl] * kscale
        vh = v_ref[0][:, sl]
        ig = zg[:, i0 + h:i0 + h + 1]
        fg = lfz[:, f0 + h:f0 + h + 1]
        m_prev = m0_ref[0][:, h:h + 1]
        cst = c0_ref[0, h]
        nst = n0_ref[0][h:h + 1, :]
        inter = fg + m_prev
        m_t = jnp.maximum(ig, inter)
        w_intra = jnp.exp(ig - m_t)
        w_inter = jnp.exp(inter - m_t)
        s = jnp.sum(qh * kh, axis=1, keepdims=True) * w_intra
        q8 = jnp.broadcast_to(qh, (8, md))
        cq = lax.dot_general(q8, cst, NT_DIMS, precision=HIGHEST,
                             preferred_element_type=F32)[0:1, :]
        num = s * vh + w_inter * cq
        den = s + w_inter * jnp.sum(nst * qh, axis=1, keepdims=True)
        hh = num / jnp.maximum(jnp.abs(den), jnp.exp(-m_t))
        og = _sigmoid(o_ref[0][:, sl])
        ym_ref[0, :, sl] = _head_norm(og * hh, gain_ref[:, sl])
        m_new = jnp.maximum(inter, ig)
        decay = jnp.exp(inter - m_new)
        wa = jnp.exp(ig - m_new)
        c1_ref[0, h] = decay * cst + (vcol_ref[0, h] * wa) * kh
        n1_ref[0, h:h + 1, :] = decay * nst + wa * kh
        m_row = jnp.where(lane == h, m_new, m_row)
    m1_ref[0] = m_row


def _sample_mix(zs, zgs, conv_w, gain, state_conv, c0, n0, m0, layer, cols, heads, md, i0, f0):
    db = zs.shape[0]
    cw = state_conv.shape[-1]
    mw = heads * md
    kk = conv_w.shape[1]
    cb0, q0 = cols
    seg = lambda a, width: zs[:, a:a + width].reshape(db, 1, width)
    cb, cc, ch = seg(cb0, cw), seg(cb0 + cw, cw), seg(cb0 + 2 * cw, cw)
    q, k, v, o = (seg(q0 + i * mw, mw) for i in range(4))
    vcol = v.reshape(db, heads, md, 1)
    zg3 = zgs.reshape(db, 1, LANES)
    m03 = m0.reshape(m0.shape[0], db, 1, heads)
    row3 = lambda width: pl.BlockSpec((1, 1, width), lambda b: (b, 0, 0))
    in_specs = [row3(cw), row3(cw), row3(cw),
                pl.BlockSpec((None, kk, cw), lambda b: (layer, 0, 0)),
                pl.BlockSpec((None, 1, kk - 1, cw), lambda b: (layer, b, 0, 0)),
                row3(mw), row3(mw), row3(mw),
                pl.BlockSpec((1, heads, md, 1), lambda b: (b, 0, 0, 0)),
                row3(mw), row3(LANES),
                pl.BlockSpec((None, 1, mw), lambda b: (layer, 0, 0)),
                pl.BlockSpec((None, 1, heads, md, md), lambda b: (layer, b, 0, 0, 0)),
                pl.BlockSpec((None, 1, heads, md), lambda b: (layer, b, 0, 0)),
                pl.BlockSpec((None, 1, 1, heads), lambda b: (layer, b, 0, 0))]
    out_specs = [row3(cw),
                 pl.BlockSpec((1, kk - 1, cw), lambda b: (b, 0, 0)),
                 row3(mw),
                 pl.BlockSpec((1, heads, md, md), lambda b: (b, 0, 0, 0)),
                 pl.BlockSpec((1, heads, md), lambda b: (b, 0, 0)),
                 pl.BlockSpec((1, 1, heads), lambda b: (b, 0, 0)),
                 row3(LANES)]
    out_shape = [jax.ShapeDtypeStruct((db, 1, cw), F32),
                 jax.ShapeDtypeStruct((db, kk - 1, cw), F32),
                 jax.ShapeDtypeStruct((db, 1, mw), F32),
                 jax.ShapeDtypeStruct((db, heads, md, md), F32),
                 jax.ShapeDtypeStruct((db, heads, md), F32),
                 jax.ShapeDtypeStruct((db, 1, heads), F32),
                 jax.ShapeDtypeStruct((db, 1, LANES), F32)]
    need = 4 * heads * md * md * 4 + 2 * heads * md * LANES * 4
    return pl.pallas_call(
        functools.partial(_sample_mix_kernel, heads=heads, md=md, i0=i0, f0=f0),
        grid=(db,),
        in_specs=in_specs,
        out_specs=out_specs,
        out_shape=out_shape,
        compiler_params=_params(("arbitrary",), need + (8 << 20)),
        name="sample_mix",
    )(cb, cc, ch, conv_w, state_conv, q, k, v, vcol, o, zg3, gain, c0, n0, m03)


def _put_rows_kernel(dst_ref, rows_ref, o_ref):
    del dst_ref
    o_ref[...] = rows_ref[...]


def _put_tail_rows(dst, rows):
    n, w = dst.shape
    last = n // ROW_TILE - 1
    return pl.pallas_call(
        _put_rows_kernel,
        grid=(1,),
        in_specs=[pl.BlockSpec(memory_space=pl.ANY),
                  pl.BlockSpec((ROW_TILE, w), lambda i: (0, 0))],
        out_specs=pl.BlockSpec((ROW_TILE, w), lambda i: (last, 0)),
        out_shape=jax.ShapeDtypeStruct(dst.shape, dst.dtype),
        input_output_aliases={0: 0},
        name="put_tail_rows",
    )(dst, rows)


def _layer_norm_rows(y, g, b):
    mu = jnp.mean(y, axis=1, keepdims=True)
    yc = y - mu
    var = jnp.mean(yc * yc, axis=1, keepdims=True)
    return yc * lax.rsqrt(var + LN_EPS) * g + b


def _first_argmax(x, width):
    lane = lax.broadcasted_iota(jnp.int32, x.shape, 1).astype(F32)
    mx = jnp.max(x, axis=1, keepdims=True)
    idx = jnp.min(jnp.where(x == mx, lane, float(width)), axis=1, keepdims=True)
    return mx, idx


def _ln_router_kernel(y_ref, g_ref, b_ref, wr_ref, br_ref, x_ref, r_ref, *, groups, epg):
    x1 = _layer_norm_rows(y_ref[...], g_ref[...], b_ref[...])
    x_ref[...] = x1
    w = wr_ref[...]
    xh = x1.astype(BF16)
    xl = (x1 - xh.astype(F32)).astype(BF16)
    wh = w.astype(BF16)
    wl = (w - wh.astype(F32)).astype(BF16)
    logits = (jnp.dot(xh, wh, preferred_element_type=F32) + jnp.dot(xh, wl, preferred_element_type=F32)
              + jnp.dot(xl, wh, preferred_element_type=F32) + br_ref[...])
    gl = logits[:, 0:groups]
    gmax, gstar = _first_argmax(gl, groups)
    p_star = 1.0 / jnp.sum(jnp.exp(gl - gmax), axis=1, keepdims=True)
    elg = logits[:, groups:groups + epg]
    for g in range(1, groups):
        elg = jnp.where(gstar == g, logits[:, groups + g * epg:groups + (g + 1) * epg], elg)
    v0, i0 = _first_argmax(elg, epg)
    lane = lax.broadcasted_iota(jnp.int32, elg.shape, 1).astype(F32)
    v1, i1 = _first_argmax(jnp.where(lane == i0, NEG_INF, elg), epg)
    e1 = jnp.exp(v1 - v0)
    w0 = p_star / (1.0 + e1)
    w1 = p_star * e1 / (1.0 + e1)
    out_lane = lax.broadcasted_iota(jnp.int32, r_ref.shape, 1)
    ex0 = (gstar * epg + i0).astype(F32)
    ex1 = (gstar * epg + i1).astype(F32)
    r = jnp.where(out_lane == 0, ex0, 0.0)
    r = jnp.where(out_lane == 1, ex1, r)
    r = jnp.where(out_lane == 2, w0, r)
    r = jnp.where(out_lane == 3, w1, r)
    r_ref[...] = r


def _ln_router(y, g, b, wr, br, layer, groups, epg):
    n, d = y.shape
    tm = LN_ROW_TILE
    need = 2 * tm * d * 4 * 2 + 2 * d * LANES * 4 + 6 * tm * d * 4
    vec = pl.BlockSpec((None, 1, d), lambda i: (layer, 0, 0))
    return pl.pallas_call(
        functools.partial(_ln_router_kernel, groups=groups, epg=epg),
        grid=(n // tm,),
        in_specs=[pl.BlockSpec((tm, d), lambda i: (i, 0)), vec, vec,
                  pl.BlockSpec((d, LANES), lambda i: (0, 0)),
                  pl.BlockSpec((1, LANES), lambda i: (0, 0))],
        out_specs=[pl.BlockSpec((tm, d), lambda i: (i, 0)),
                   pl.BlockSpec((tm, LANES), lambda i: (i, 0))],
        out_shape=[jax.ShapeDtypeStruct((n, d), F32),
                   jax.ShapeDtypeStruct((n, LANES), F32)],
        compiler_params=_params(("arbitrary",), need + (4 << 20)),
        name="ln_router",
    )(y, g, b, wr, br)


def _moe_gather_kernel(tok_ref, nv_ref, x_hbm, o_ref, buf_ref, sem_ref, *, tm):
    t = pl.program_id(0)
    nv = nv_ref[0]
    slot = t % 2

    def row_copy(tok, k, s):
        return pltpu.make_async_copy(x_hbm.at[pl.ds(tok, 1), :], buf_ref.at[s, pl.ds(k, 1), :],
                                     sem_ref.at[s])

    def issue(tile, s):
        def body(k, carry):
            row_copy(tok_ref[tile * tm + k], k, s).start()
            return carry

        lax.fori_loop(0, tm, body, 0, unroll=8)

    @pl.when(t == 0)
    def _():
        issue(0, 0)

    @pl.when(t + 1 < nv)
    def _():
        issue(t + 1, 1 - slot)

    @pl.when(t < nv)
    def _():
        def body(k, carry):
            row_copy(0, k, slot).wait()
            return carry

        lax.fori_loop(0, tm, body, 0, unroll=8)
        o_ref[...] = buf_ref[slot].astype(BF16)


def _moe_gather(row_token, n_valid, x1, tm):
    d = x1.shape[1]
    r = row_token.shape[0]
    grid_spec = pltpu.PrefetchScalarGridSpec(
        num_scalar_prefetch=2,
        grid=(r // tm,),
        in_specs=[pl.BlockSpec(memory_space=pl.ANY)],
        out_specs=pl.BlockSpec((tm, d), lambda t, tok, nv: (jnp.minimum(t, nv[0] - 1), 0)),
        scratch_shapes=[pltpu.VMEM((2, tm, d), F32), pltpu.SemaphoreType.DMA((2,))],
    )
    need = 2 * tm * d * 4 + 2 * tm * d * 2 + 2 * tm * d * 4
    return pl.pallas_call(
        functools.partial(_moe_gather_kernel, tm=tm),
        grid_spec=grid_spec,
        out_shape=jax.ShapeDtypeStruct((r, d), BF16),
        compiler_params=_params(("arbitrary",), need + (4 << 20)),
        name="moe_gather",
    )(row_token, n_valid, x1)


def _moe_in_kernel(te_ref, nv_ref, first_ref, run_ref, nexte_ref, lastrun_ref, nruns_ref,
                   x_ref, w_hbm, a_ref, wbuf_ref, wgb_ref, wvb_ref, sem_ref,
                   *, layer, tf, nf, chunk):
    j = pl.program_id(0)
    t = pl.program_id(1)
    valid = t < nv_ref[0]
    c = j * nruns_ref[0] + run_ref[t]
    slot = c % 2
    starts_run = jnp.logical_and(valid, first_ref[t] == 1)

    def fetch(e, jj, s):
        col_g = pl.multiple_of(jj * tf, tf)
        col_v = pl.multiple_of((nf + jj) * tf, tf)
        return (pltpu.make_async_copy(w_hbm.at[layer, e, :, pl.ds(col_g, tf)], wbuf_ref.at[s, 0],
                                      sem_ref.at[s, 0]),
                pltpu.make_async_copy(w_hbm.at[layer, e, :, pl.ds(col_v, tf)], wbuf_ref.at[s, 1],
                                      sem_ref.at[s, 1]))

    @pl.when(jnp.logical_and(starts_run, c == 0))
    def _():
        for cp in fetch(te_ref[t], j, 0):
            cp.start()

    @pl.when(starts_run)
    def _():
        for cp in fetch(te_ref[t], j, slot):
            cp.wait()
        _cast_rows(wbuf_ref.at[slot, 0], wgb_ref, chunk)
        _cast_rows(wbuf_ref.at[slot, 1], wvb_ref, chunk)
        last_run = lastrun_ref[t] == 1

        @pl.when(jnp.logical_not(last_run))
        def _():
            for cp in fetch(nexte_ref[t], j, 1 - slot):
                cp.start()

        @pl.when(jnp.logical_and(last_run, j + 1 < nf))
        def _():
            for cp in fetch(te_ref[0], j + 1, 1 - slot):
                cp.start()

    @pl.when(valid)
    def _():
        x = x_ref[...]
        hg = jnp.dot(x, wgb_ref[...], preferred_element_type=F32)
        hv = jnp.dot(x, wvb_ref[...], preferred_element_type=F32)
        a_ref[...] = (hg * _sigmoid(hg) * hv).astype(a_ref.dtype)


def _moe_in(sched, xs, w_e_in, layer, de, tm):
    r, d = xs.shape
    tf = _pick(de, (512, 256, 128))
    nf = de // tf
    n_tiles = r // tm
    need = 2 * 2 * d * tf * 4 + 2 * d * tf * 2 + 2 * tm * d * 2 + 6 * tm * tf * 4

    def row(j, t, te, nv, *_):
        return (jnp.minimum(t, nv[0] - 1), 0)

    def out_map(j, t, te, nv, *_):
        return (jnp.minimum(t, nv[0] - 1), j)

    grid_spec = pltpu.PrefetchScalarGridSpec(
        num_scalar_prefetch=len(sched),
        grid=(nf, n_tiles),
        in_specs=[pl.BlockSpec((tm, d), row), pl.BlockSpec(memory_space=pl.ANY)],
        out_specs=pl.BlockSpec((tm, tf), out_map),
        scratch_shapes=[pltpu.VMEM((2, 2, d, tf), F32), pltpu.VMEM((d, tf), BF16),
                        pltpu.VMEM((d, tf), BF16), pltpu.SemaphoreType.DMA((2, 2))],
    )
    return pl.pallas_call(
        functools.partial(_moe_in_kernel, layer=layer, tf=tf, nf=nf, chunk=_pick(d, (256, 128, 8))),
        grid_spec=grid_spec,
        out_shape=jax.ShapeDtypeStruct((r, de), BF16),
        compiler_params=_params(("arbitrary", "arbitrary"), need + (4 << 20)),
        name="moe_in",
    )(*sched, xs, w_e_in)


def _moe_down_kernel(te_ref, nv_ref, first_ref, run_ref, nexte_ref, lastrun_ref, nruns_ref, dest_ref,
                     a_ref, w_hbm, out_hbm, wbuf_ref, wb_ref, y_ref, wsem_ref, sem_ref,
                     *, layer, chunk, tm):
    del nruns_ref
    t = pl.program_id(0)
    last = pl.num_programs(0) - 1
    nv = nv_ref[0]
    slot = t % 2
    wslot = run_ref[t] % 2
    starts_run = jnp.logical_and(t < nv, first_ref[t] == 1)

    def fetch(e, s):
        return pltpu.make_async_copy(w_hbm.at[layer, e], wbuf_ref.at[s], wsem_ref.at[s])

    def row_copy(k, s, dest):
        return pltpu.make_async_copy(y_ref.at[s, pl.ds(k, 1), :], out_hbm.at[pl.ds(dest, 1), :],
                                     sem_ref.at[s])

    def drain(s):
        def body(k, carry):
            row_copy(k, s, 0).wait()
            return carry

        lax.fori_loop(0, tm, body, 0, unroll=8)

    @pl.when(jnp.logical_and(t >= 2, t - 2 < nv))
    def _():
        drain(slot)

    @pl.when(jnp.logical_and(starts_run, t == 0))
    def _():
        fetch(te_ref[t], 0).start()

    @pl.when(starts_run)
    def _():
        fetch(te_ref[t], wslot).wait()
        _cast_rows(wbuf_ref.at[wslot], wb_ref, chunk)

        @pl.when(lastrun_ref[t] == 0)
        def _():
            fetch(nexte_ref[t], 1 - wslot).start()

    @pl.when(t < nv)
    def _():
        y_ref[slot] = jnp.dot(a_ref[...], wb_ref[...], preferred_element_type=F32)

        def body(k, carry):
            row_copy(k, slot, dest_ref[t * tm + k]).start()
            return carry

        lax.fori_loop(0, tm, body, 0, unroll=8)

    @pl.when(jnp.logical_and(t == last, jnp.logical_and(t >= 1, t - 1 < nv)))
    def _():
        drain(1 - slot)

    @pl.when(jnp.logical_and(t == last, t < nv))
    def _():
        drain(slot)


def _moe_down(sched, row_dest, a, w_e_down, layer, n_out_rows, tm):
    r, de = a.shape
    d = w_e_down.shape[3]
    n_tiles = r // tm
    need = 2 * de * d * 4 + de * d * 2 + 2 * tm * de * 2 + 3 * tm * d * 4

    def row(t, te, nv, *_):
        return (jnp.minimum(t, nv[0] - 1), 0)

    grid_spec = pltpu.PrefetchScalarGridSpec(
        num_scalar_prefetch=len(sched) + 1,
        grid=(n_tiles,),
        in_specs=[pl.BlockSpec((tm, de), row), pl.BlockSpec(memory_space=pl.ANY)],
        out_specs=pl.BlockSpec(memory_space=pl.ANY),
        scratch_shapes=[pltpu.VMEM((2, de, d), F32), pltpu.VMEM((de, d), BF16),
                        pltpu.VMEM((2, tm, d), F32),
                        pltpu.SemaphoreType.DMA((2,)), pltpu.SemaphoreType.DMA((2,))],
    )
    return pl.pallas_call(
        functools.partial(_moe_down_kernel, layer=layer, chunk=_pick(de, (256, 128, 8)), tm=tm),
        grid_spec=grid_spec,
        out_shape=jax.ShapeDtypeStruct((n_out_rows, d), F32),
        compiler_params=_params(("arbitrary",), need + (4 << 20)),
        name="moe_down",
    )(*sched, row_dest, a, w_e_down)


def _combine_ln_kernel(x_ref, y0_ref, y1_ref, r_ref, g_ref, b_ref, *outs, alpha, n_main, final):
    r = r_ref[...]
    y = alpha * x_ref[...] + r[:, 2:3] * y0_ref[...] + r[:, 3:4] * y1_ref[...]
    x2 = _layer_norm_rows(y, g_ref[...], b_ref[...])
    if final:
        main_ref, tail_ref = outs
        i = pl.program_id(0)

        @pl.when(i < n_main)
        def _():
            main_ref[...] = x2

        @pl.when(i >= n_main)
        def _():
            tail_ref[...] = x2
    else:
        o_ref, ob_ref = outs
        o_ref[...] = x2
        ob_ref[...] = x2.astype(BF16)


def _combine_ln(x1, y2, route, g, b, layer, alpha, n_p, final):
    n, d = x1.shape
    tm = LN_ROW_TILE
    steps = n // tm
    n_main = n_p // tm
    need = 2 * 4 * tm * d * 4 + 2 * tm * d * 2 + 4 * tm * d * 4
    blk = pl.BlockSpec((tm, d), lambda i: (i, 0))
    vec = pl.BlockSpec((None, 1, d), lambda i: (layer, 0, 0))
    if final:
        out_specs = [pl.BlockSpec((tm, d), lambda i: (jnp.minimum(i, n_main - 1), 0)),
                     pl.BlockSpec((tm, d), lambda i: (jnp.maximum(i - n_main, 0), 0))]
        out_shape = [jax.ShapeDtypeStruct((n_p, d), F32), jax.ShapeDtypeStruct((n - n_p, d), F32)]
    else:
        out_specs = [blk, blk]
        out_shape = [jax.ShapeDtypeStruct((n, d), F32), jax.ShapeDtypeStruct((n, d), BF16)]
    return pl.pallas_call(
        functools.partial(_combine_ln_kernel, alpha=alpha, n_main=n_main, final=final),
        grid=(steps,),
        in_specs=[blk, blk, pl.BlockSpec((tm, d), lambda i: (steps + i, 0)),
                  pl.BlockSpec((tm, LANES), lambda i: (i, 0)), vec, vec],
        out_specs=out_specs,
        out_shape=out_shape,
        compiler_params=_params(("arbitrary",), need + (4 << 20)),
        name="combine_ln",
    )(x1, y2, y2, route, g, b)


def _expert_rows(route, n_experts, tm):
    n_tok = route.shape[0]
    e = route[:, 0:2].astype(jnp.int32).reshape(-1)
    n_pairs = e.shape[0]
    onehot = (e[:, None] == jnp.arange(n_experts, dtype=jnp.int32)[None, :]).astype(jnp.int32)
    running = jnp.cumsum(onehot, axis=0)
    counts = running[-1]
    rank = jnp.sum(running * onehot, axis=1) - 1
    padded = (counts + tm - 1) // tm * tm
    ends = jnp.cumsum(padded)
    starts = ends - padded
    pos = jnp.sum(onehot * starts[None, :], axis=1) + rank
    n_rows = (n_pairs + n_experts * (tm - 1)) // tm * tm
    n_tiles = n_rows // tm
    row_pair = jnp.full((n_rows,), -1, jnp.int32).at[pos].set(jnp.arange(n_pairs, dtype=jnp.int32))
    rows = jnp.arange(n_rows, dtype=jnp.int32)
    real = row_pair >= 0
    row_token = jnp.where(real, row_pair // 2, 0)
    spare = 2 * n_tok + (rows // tm % 2) * tm + rows % tm
    row_dest = jnp.where(real, (row_pair % 2) * n_tok + row_pair // 2, spare)
    n_valid = (ends[-1] // tm).astype(jnp.int32)
    tile_start = jnp.arange(n_tiles, dtype=jnp.int32) * tm
    tile_expert = jnp.sum((tile_start[:, None] >= ends[None, :]).astype(jnp.int32), axis=1)
    last_expert = jnp.max(jnp.where(counts > 0, jnp.arange(n_experts, dtype=jnp.int32), 0))
    tile_expert = jnp.minimum(tile_expert, last_expert).astype(jnp.int32)
    tiles = jnp.arange(n_tiles, dtype=jnp.int32)
    prev_expert = jnp.concatenate([jnp.full((1,), -1, jnp.int32), tile_expert[:-1]])
    first = jnp.logical_and(tiles < n_valid, tile_expert != prev_expert)
    run = jnp.cumsum(first.astype(jnp.int32)) - 1
    start_at_or_after = lax.cummin(jnp.where(first, tiles, n_tiles), axis=0, reverse=True)
    next_start = jnp.concatenate([start_at_or_after[1:], jnp.full((1,), n_tiles, jnp.int32)])
    last_run = next_start >= n_tiles
    next_expert = jnp.where(last_run, tile_expert[0],
                            tile_expert[jnp.minimum(next_start, n_tiles - 1)])
    n_runs = jnp.sum(first.astype(jnp.int32)).reshape(1)
    sched = (tile_expert, n_valid.reshape(1), first.astype(jnp.int32), run.astype(jnp.int32),
             next_expert.astype(jnp.int32), last_run.astype(jnp.int32), n_runs)
    return row_token, row_dest, sched


def kernel(x_prompt, x_sample, cache_k, cache_v, cache_logf, state_conv, state_mlstm_c,
           state_mlstm_n, state_mlstm_m, page_table, w_in, b_in, conv_w, mlstm_norm_w,
           w_branch, w_out, ln1_g, ln1_b, w_router_group, b_router_group, w_router_expert,
           b_router_expert, w_expert_in, w_expert_down, ln2_g, ln2_b):
    bp, t, d = x_prompt.shape
    db = x_sample.shape[0]
    depth = w_in.shape[0]
    n_pool, ps, heads, hd = cache_k.shape[1:]
    aw = heads * hd
    cw = state_conv.shape[-1]
    mh, md = state_mlstm_c.shape[2], state_mlstm_c.shape[3]
    mw = mh * md
    groups = w_router_group.shape[-1]
    n_experts = w_router_expert.shape[-1]
    epg = n_experts // groups
    de = w_expert_down.shape[2]
    alpha = float((2 * depth) ** 0.25)
    n_p = bp * t
    n_tok = n_p + db
    n_all = n_p + ROW_TILE
    moe_tm = MOE_ROW_TILE

    c_f = 3 * aw
    c_i = c_f + heads + 3 * cw + 4 * mw
    g2_cols = 3 * cw + 4 * mw
    g3_cols = 3 * d
    zc_conv = 0
    zc_m = 3 * cw
    zc_g = g2_cols
    i0, f0 = heads, heads + mh
    assert heads + 2 * mh <= LANES and groups + n_experts <= LANES

    x_main = x_prompt.reshape(n_p, d)
    x_tail = jnp.concatenate([x_sample.reshape(db, d), jnp.zeros((ROW_TILE - db, d), F32)], axis=0)
    xb_all = jnp.concatenate([x_main, x_tail], axis=0).astype(BF16)
    b_in3 = b_in.reshape(depth, 1, b_in.shape[1])
    wt_in = jnp.swapaxes(w_in, 1, 2)

    cache_k2 = cache_k.reshape(depth, n_pool, ps * heads, hd)
    cache_v2 = cache_v.reshape(depth, n_pool, ps * heads, hd)
    cache_lft = jnp.swapaxes(cache_logf, 2, 3)
    gain3 = mlstm_norm_w.reshape(depth, 1, mw)
    ln1_g3, ln1_b3 = ln1_g.reshape(depth, 1, d), ln1_b.reshape(depth, 1, d)
    ln2_g3, ln2_b3 = ln2_g.reshape(depth, 1, d), ln2_b.reshape(depth, 1, d)

    def tail(rows):
        rows = rows.astype(BF16)
        return jnp.concatenate([rows, jnp.zeros((ROW_TILE - db, rows.shape[1]), BF16)], axis=0)

    outs = [[] for _ in range(12)]
    k_all = v_all = None
    for l in range(depth):
        b_l = b_in[l]
        b_main = jnp.concatenate([b_l[c_f + heads:c_i], b_l[c_i + 2 * mh:]]).reshape(1, -1)
        b_gate = jnp.concatenate([b_l[c_f:c_f + heads], b_l[c_i:c_i + 2 * mh],
                                  jnp.zeros((LANES - heads - 2 * mh,), F32)]).reshape(1, LANES)
        qb, kvb, k_all, v_all, kv_tail = _qkv_proj(xb_all, wt_in, b_in3, k_all, v_all, l, aw, n_p)
        zc = _main_proj(xb_all, wt_in, b_main, l, c_f, g2_cols, g3_cols, heads, heads + 2 * mh)
        zg = _gate_proj(xb_all, wt_in, b_gate, l, c_f, c_i, heads, mh)

        lf_p, cum, cumt = _prompt_gates(zg, bp, t)
        ya = _prompt_attention(qb, kvb, cum, cumt, n_all, bp, t, heads, hd)
        yc, conv_p = _prompt_conv(zc, conv_w, l, zc_conv, n_all, bp, t, cw)
        ym, c_p, n_p_state, m_p = _prompt_mlstm(zc, zg, gain3, l, zc_m, n_all, bp, t, mh, md, i0, f0)

        zs = zc[n_p:n_tok]
        zgs = zg[n_p:n_tok]
        yc_s, conv_s, ym_s, c_s, n_s, m_s, lf_s = _sample_mix(
            zs, zgs, conv_w, gain3, state_conv, state_mlstm_c, state_mlstm_n, state_mlstm_m,
            l, (zc_conv, zc_m), mh, md, i0, f0)
        q_s = qb[n_p:n_tok].astype(F32).reshape(db, heads, hd)
        k_s = kv_tail[:db, 0:aw].reshape(db, heads, hd)
        v_s = kv_tail[:db, aw:2 * aw].reshape(db, heads, hd)
        lf_new = lf_s[:, 0, 0:heads]
        ya_s = _decode_attention(page_table, q_s, k_s, v_s, lf_new.reshape(db, heads, 1),
                                 cache_k2, cache_v2, cache_lft, l)
        ya = _put_tail_rows(ya, tail(ya_s.reshape(db, aw)))
        yc = _put_tail_rows(yc, tail(yc_s.reshape(db, cw)))
        ym = _put_tail_rows(ym, tail(ym_s.reshape(db, mw)))

        merged = _merge(ya, yc, ym, zc, zc_g, w_branch, l, d)
        y1 = _outproj(merged, x_main, x_tail, w_out, l, alpha)

        wr = jnp.concatenate([w_router_group[l], w_router_expert[l],
                              jnp.zeros((d, LANES - groups - n_experts), F32)], axis=1)
        br = jnp.concatenate([b_router_group[l], b_router_expert[l],
                              jnp.zeros((LANES - groups - n_experts,), F32)]).reshape(1, LANES)
        x1, route = _ln_router(y1, ln1_g3, ln1_b3, wr, br, l, groups, epg)

        row_token, row_dest, sched = _expert_rows(route, n_experts, moe_tm)
        xs = _moe_gather(row_token, sched[1], x1, moe_tm)
        act = _moe_in(sched, xs, w_expert_in, l, de, moe_tm)
        y2 = _moe_down(sched, row_dest, act, w_expert_down, l, 2 * n_all + 2 * moe_tm, moe_tm)
        final = l == depth - 1
        x_main, x_next = _combine_ln(x1, y2, route, ln2_g3, ln2_b3, l, alpha, n_p, final)
        if final:
            y_prompt = x_main.reshape(bp, t, d)
            y_sample = x_next[:db].reshape(db, 1, d)
        else:
            x_tail, xb_all = x_main, x_next

        per_layer = (
            lf_p[:, 0:heads].reshape(bp, t, heads),
            k_s.reshape(db, 1, heads, hd),
            v_s.reshape(db, 1, heads, hd),
            lf_new.reshape(db, 1, heads),
            conv_p, conv_s,
            c_p, c_s,
            n_p_state.reshape(bp, mh, md), n_s,
            m_p.reshape(bp, mh), m_s.reshape(db, mh),
        )
        for acc, val in zip(outs, per_layer):
            acc.append(val)

    stacked = [jnp.stack(a, axis=0) for a in outs]
    return (y_prompt, y_sample,
            k_all.reshape(depth, bp, t, heads, hd), v_all.reshape(depth, bp, t, heads, hd),
            *stacked)
```
